```python
import math
import jax, jax.numpy as jnp
from jax import lax
import numpy as np

D_MODEL = 2048
BATCH = 4
SEQ = 2048
DEPTH = 2

HEAD_DIM = 128
ATTN_SLOTS = 4
DILATED_PATTERNS = ((128, 1), (512, 4), (2048, 16))
N_GROUPS = len(DILATED_PATTERNS)
N_ATTN_HEADS = N_GROUPS * ATTN_SLOTS
ATTN_WIDTH = N_ATTN_HEADS * HEAD_DIM
ATTN_OUT = ATTN_SLOTS * HEAD_DIM
ROPE_THETA = 500000.0
ROPE_DIM = HEAD_DIM // 4
CONV_WIDTH = D_MODEL // 2
CONV_KERNEL = 31
GMLP_WIDTH = D_MODEL // 2
GMLP_CHUNK = 128
GMLP_GROUPS = 8
GMLP_GROUP_CH = GMLP_WIDTH // GMLP_GROUPS
FFN_HIDDEN = -(-8 * D_MODEL // (3 * 256)) * 256
DEEPNORM_ALPHA = (2 * DEPTH) ** 0.25
DEEPNORM_BETA = (8 * DEPTH) ** -0.25
LN_EPS = 1e-5
NEG_INF = -1e30
IN_SIZES = (ATTN_WIDTH, ATTN_WIDTH, ATTN_WIDTH, 2 * CONV_WIDTH, 2 * GMLP_WIDTH, 3 * D_MODEL)
IN_WIDTH = sum(IN_SIZES)
IN_SPLITS = tuple(sum(IN_SIZES[:i + 1]) for i in range(len(IN_SIZES) - 1))

kernel_name = "hybrid_dilated_conv_sgu_gated_deepnorm"


def layer_norm(x, g, b):
    xf = x.astype(jnp.float32)
    mu = xf.mean(-1, keepdims=True)
    var = jnp.square(xf - mu).mean(-1, keepdims=True)
    return ((xf - mu) * lax.rsqrt(var + LN_EPS) * g.astype(jnp.float32)
            + b.astype(jnp.float32)).astype(x.dtype)


def partial_rotary(x, pos):
    half = ROPE_DIM // 2
    inv_freq = ROPE_THETA ** (-jnp.arange(half, dtype=jnp.float32) / half)
    ang = pos.astype(jnp.float32)[:, None] * inv_freq[None, :]
    cos = jnp.cos(ang)[None, :, None, :]
    sin = jnp.sin(ang)[None, :, None, :]
    xr = x[..., :ROPE_DIM].astype(jnp.float32)
    x1, x2 = xr[..., :half], xr[..., half:]
    rot = jnp.concatenate([x1 * cos - x2 * sin, x2 * cos + x1 * sin], -1).astype(x.dtype)
    return jnp.concatenate([rot, x[..., ROPE_DIM:]], -1)


def dilated_window_attention(q, k, v, dilation, band):
    B, S, H, hd = q.shape
    L = S // dilation
    N = B * dilation

    def to_residue(t):
        return t.reshape(B, L, dilation, H, hd).transpose(0, 2, 1, 3, 4).reshape(N, L, H, hd)

    qr, kr, vr = to_residue(q), to_residue(k), to_residue(v)
    nb = -(-L // band)
    Lp = nb * band
    pad = Lp - L
    qr = jnp.pad(qr, ((0, 0), (0, pad), (0, 0), (0, 0)))
    kr = jnp.pad(kr, ((0, 0), (band, pad), (0, 0), (0, 0)))
    vr = jnp.pad(vr, ((0, 0), (band, pad), (0, 0), (0, 0)))
    qb = qr.reshape(N, nb, band, H, hd)
    kb = kr.reshape(N, nb + 1, band, H, hd)
    vb = vr.reshape(N, nb + 1, band, H, hd)
    kw = jnp.concatenate([kb[:, :-1], kb[:, 1:]], axis=2)
    vw = jnp.concatenate([vb[:, :-1], vb[:, 1:]], axis=2)
    scores = jnp.einsum('nbqhd,nbkhd->nbhqk', qb, kw).astype(jnp.float32) * (hd ** -0.5)
    i = jnp.arange(band)[None, :, None]
    j = jnp.arange(2 * band)[None, None, :]
    key_pos = jnp.arange(nb)[:, None, None] * band - band + j
    valid = (j >= i) & (j <= i + band) & (key_pos >= 0)
    scores = jnp.where(valid[None, :, None], scores, NEG_INF)
    m = scores.max(-1, keepdims=True)
    p = jnp.exp(scores - m)
    denom = p.sum(-1, keepdims=True)
    out = jnp.einsum('nbhqk,nbkhd->nbqhd', (p / denom).astype(v.dtype), vw)
    lse = (m + jnp.log(denom))[..., 0]
    out = (out.reshape(N, Lp, H, hd)[:, :L]
           .reshape(B, dilation, L, H, hd).transpose(0, 2, 1, 3, 4).reshape(B, S, H, hd))
    lse = (lse.transpose(0, 1, 3, 2).reshape(N, Lp, H)[:, :L]
           .reshape(B, dilation, L, H).transpose(0, 2, 1, 3).reshape(B, S, H))
    return out, lse


def conformer_conv(a, w_dw, b_dw, ln_g, ln_b, w_proj):
    val, gate = jnp.split(a, 2, -1)
    h = val * jax.nn.sigmoid(gate)
    C = h.shape[-1]
    h = jnp.pad(h, ((0, 0), (CONV_KERNEL - 1, 0), (0, 0)))
    h = lax.conv_general_dilated(h, w_dw[:, None, :].astype(h.dtype), window_strides=(1,),
                                 padding='VALID', dimension_numbers=('NWC', 'WIO', 'NWC'),
                                 feature_group_count=C) + b_dw
    h = jax.nn.silu(layer_norm(h, ln_g, ln_b))
    return h @ w_proj


def chunked_spatial_gating(z, ln_g, ln_b, w_s, b_s, w_proj):
    z = jax.nn.gelu(z, approximate=False)
    u, v = jnp.split(z, 2, -1)
    v = layer_norm(v, ln_g, ln_b)
    B, S, _ = v.shape
    nc = S // GMLP_CHUNK
    vc = v.reshape(B, nc, GMLP_CHUNK, GMLP_GROUPS, GMLP_GROUP_CH)
    causal = jnp.tril(jnp.ones((GMLP_CHUNK, GMLP_CHUNK), dtype=bool))
    w_m = jnp.where(causal[None], w_s, 0).astype(v.dtype)
    s = jnp.einsum('gts,bnsgc->bntgc', w_m, vc) + b_s.T[None, None, :, :, None]
    return (u * s.reshape(B, S, GMLP_WIDTH)) @ w_proj


def hybrid_mixer(x, pos, w_in, w_attn_proj, conv_dw, conv_dw_b, conv_ln_g, conv_ln_b,
                 w_conv_proj, gmlp_ln_g, gmlp_ln_b, w_spatial, b_spatial, w_gmlp_proj, w_out):
    B, S, _ = x.shape
    proj = x @ w_in
    q, k, v, conv_in, gmlp_in, gates = jnp.split(proj, IN_SPLITS, axis=-1)
    q = partial_rotary(q.reshape(B, S, N_ATTN_HEADS, HEAD_DIM), pos)
    k = partial_rotary(k.reshape(B, S, N_ATTN_HEADS, HEAD_DIM), pos)
    v = v.reshape(B, S, N_ATTN_HEADS, HEAD_DIM)
    outs, lses = [], []
    for g, (window, dilation) in enumerate(DILATED_PATTERNS):
        sl = slice(g * ATTN_SLOTS, (g + 1) * ATTN_SLOTS)
        o, l = dilated_window_attention(q[:, :, sl], k[:, :, sl], v[:, :, sl],
                                        dilation, window // dilation)
        outs.append(o)
        lses.append(l)
    wts = jax.nn.softmax(jnp.stack(lses, 0), axis=0)
    attn = jnp.sum(wts[..., None].astype(v.dtype) * jnp.stack(outs, 0), axis=0)
    y_attn = attn.reshape(B, S, ATTN_OUT) @ w_attn_proj
    y_conv = conformer_conv(conv_in, conv_dw, conv_dw_b, conv_ln_g, conv_ln_b, w_conv_proj)
    y_gmlp = chunked_spatial_gating(gmlp_in, gmlp_ln_g, gmlp_ln_b, w_spatial, b_spatial, w_gmlp_proj)
    g_a, g_c, g_m = jnp.split(jax.nn.sigmoid(gates), 3, axis=-1)
    merged = g_a * y_attn + g_c * y_conv + g_m * y_gmlp
    return merged @ w_out


def swiglu(x, w_gate, w_up, w_down):
    return (jax.nn.silu(x @ w_gate) * (x @ w_up)) @ w_down


def setup_inputs(seed: int = 0) -> dict:
    key = jax.random.key(seed)
    ks = jax.random.split(key, 22)

    def nrm(k, shape, scale):
        return jax.random.normal(k, shape, dtype=jnp.float32) * scale

    L = DEPTH
    return {
        "x": nrm(ks[0], (BATCH, SEQ, D_MODEL), 1.0),
        "w_in": nrm(ks[1], (L, D_MODEL, IN_WIDTH), D_MODEL ** -0.5),
        "w_attn_proj": nrm(ks[2], (L, ATTN_OUT, D_MODEL), ATTN_OUT ** -0.5),
        "conv_dw": nrm(ks[3], (L, CONV_KERNEL, CONV_WIDTH), CONV_KERNEL ** -0.5),
        "conv_dw_b": nrm(ks[4], (L, CONV_WIDTH), 0.02),
        "conv_ln_g": 1.0 + nrm(ks[5], (L, CONV_WIDTH), 0.02),
        "conv_ln_b": nrm(ks[6], (L, CONV_WIDTH), 0.02),
        "w_conv_proj": nrm(ks[7], (L, CONV_WIDTH, D_MODEL), CONV_WIDTH ** -0.5),
        "gmlp_ln_g": 1.0 + nrm(ks[8], (L, GMLP_WIDTH), 0.02),
        "gmlp_ln_b": nrm(ks[9], (L, GMLP_WIDTH), 0.02),
        "w_spatial": nrm(ks[10], (L, GMLP_GROUPS, GMLP_CHUNK, GMLP_CHUNK), GMLP_CHUNK ** -0.5),
        "b_spatial": 1.0 + nrm(ks[11], (L, GMLP_GROUPS, GMLP_CHUNK), 0.02),
        "w_gmlp_proj": nrm(ks[12], (L, GMLP_WIDTH, D_MODEL), GMLP_WIDTH ** -0.5),
        "w_out": nrm(ks[13], (L, D_MODEL, D_MODEL), D_MODEL ** -0.5 * DEEPNORM_BETA),
        "ln1_g": 1.0 + nrm(ks[14], (L, D_MODEL), 0.02),
        "ln1_b": nrm(ks[15], (L, D_MODEL), 0.02),
        "w_ffn_gate": nrm(ks[16], (L, D_MODEL, FFN_HIDDEN), D_MODEL ** -0.5),
        "w_ffn_up": nrm(ks[17], (L, D_MODEL, FFN_HIDDEN), D_MODEL ** -0.5),
        "w_ffn_down": nrm(ks[18], (L, FFN_HIDDEN, D_MODEL), FFN_HIDDEN ** -0.5 * DEEPNORM_BETA),
        "ln2_g": 1.0 + nrm(ks[19], (L, D_MODEL), 0.02),
        "ln2_b": nrm(ks[20], (L, D_MODEL), 0.02),
    }


def reference(x, w_in, w_attn_proj, conv_dw, conv_dw_b, conv_ln_g, conv_ln_b, w_conv_proj,
              gmlp_ln_g, gmlp_ln_b, w_spatial, b_spatial, w_gmlp_proj, w_out, ln1_g, ln1_b,
              w_ffn_gate, w_ffn_up, w_ffn_down, ln2_g, ln2_b):
    pos = jnp.arange(x.shape[1], dtype=jnp.int32)
    for l in range(DEPTH):
        y = hybrid_mixer(x, pos, w_in[l], w_attn_proj[l], conv_dw[l], conv_dw_b[l],
                         conv_ln_g[l], conv_ln_b[l], w_conv_proj[l], gmlp_ln_g[l], gmlp_ln_b[l],
                         w_spatial[l], b_spatial[l], w_gmlp_proj[l], w_out[l])
        x = layer_norm(DEEPNORM_ALPHA * x + y, ln1_g[l], ln1_b[l])
        f = swiglu(x, w_ffn_gate[l], w_ffn_up[l], w_ffn_down[l])
        x = layer_norm(DEEPNORM_ALPHA * x + f, ln2_g[l], ln2_b[l])
    return x
```

```python
import functools
import math

import jax
import jax.numpy as jnp
from jax import lax
from jax.experimental import pallas as pl
from jax.experimental.pallas import tpu as pltpu

D_MODEL = 2048
BATCH = 4
SEQ = 2048
DEPTH = 2
TOKENS = BATCH * SEQ
HEAD_DIM = 128
ATTN_SLOTS = 4
DILATED_PATTERNS = ((128, 1), (512, 4), (2048, 16))
N_GROUPS = len(DILATED_PATTERNS)
N_ATTN_HEADS = N_GROUPS * ATTN_SLOTS
ATTN_WIDTH = N_ATTN_HEADS * HEAD_DIM
ATTN_OUT = ATTN_SLOTS * HEAD_DIM
ROPE_THETA = 500000.0
ROPE_DIM = HEAD_DIM // 4
ROPE_HALF = ROPE_DIM // 2
CONV_WIDTH = D_MODEL // 2
CONV_KERNEL = 31
GMLP_WIDTH = D_MODEL // 2
GMLP_CHUNK = 128
GMLP_GROUPS = 8
GMLP_GROUP_CH = GMLP_WIDTH // GMLP_GROUPS
FFN_HIDDEN = -(-8 * D_MODEL // (3 * 256)) * 256
DEEPNORM_ALPHA = (2 * DEPTH) ** 0.25
LN_EPS = 1e-5
NEG_INF = -1e30
IN_WIDTH = 3 * ATTN_WIDTH + 2 * CONV_WIDTH + 2 * GMLP_WIDTH + 3 * D_MODEL

LANES = 128
V7X_VMEM_LIMIT_BYTES = 56 * 1024 * 1024

PROJ_TN = 512
PROJ_TM = 1024
N_PROJ_TILES = IN_WIDTH // PROJ_TN
QKV_TILES = 3 * ATTN_WIDTH // PROJ_TN
P_CONV = 0
P_GMLP = P_CONV + 2 * CONV_WIDTH // PROJ_TN
P_GATES = P_GMLP + 2 * GMLP_WIDTH // PROJ_TN
P_QKV = P_GATES + 3 * D_MODEL // PROJ_TN
P_V = P_QKV + 2 * ATTN_WIDTH // PROJ_TN
ATTN_TQ = 256
CONV_TS = 256
CONV_HALO = 32
CONV_ROWS = 64
SGU_TM = 512
MERGE_TM = 256
FFN_TN = 512
DOWN_TM = 256

_F32 = jnp.float32
_BF16 = jnp.bfloat16


def _params(semantics):
    return pltpu.CompilerParams(dimension_semantics=semantics,
                                vmem_limit_bytes=V7X_VMEM_LIMIT_BYTES)


def _layer_norm(x, g, b):
    mu = jnp.mean(x, axis=-1, keepdims=True)
    xc = x - mu
    var = jnp.mean(xc * xc, axis=-1, keepdims=True)
    return xc * lax.rsqrt(var + LN_EPS) * g + b


def _gelu(x):
    return 0.5 * x * (1.0 + lax.erf(x * math.sqrt(0.5)))


def _rope_table_kernel(cos_ref, sin_lo_ref, sin_hi_ref):
    pos = lax.broadcasted_iota(jnp.int32, (SEQ, LANES), 0).astype(_F32)
    lane = lax.broadcasted_iota(jnp.int32, (SEQ, LANES), 1)
    fidx = (lane & (ROPE_HALF - 1)).astype(_F32)
    inv_freq = jnp.exp(fidx * (-math.log(ROPE_THETA) / ROPE_HALF))
    ang = pos * inv_freq
    c = jnp.cos(ang)
    s = jnp.sin(ang)
    cos_ref[...] = jnp.where(lane < ROPE_DIM, c, 1.0)
    sin_lo_ref[...] = jnp.where(lane < ROPE_HALF, -s, 0.0)
    sin_hi_ref[...] = jnp.where((lane >= ROPE_HALF) & (lane < ROPE_DIM), s, 0.0)


def _rope_tables():
    shp = jax.ShapeDtypeStruct((SEQ, LANES), _F32)
    return pl.pallas_call(_rope_table_kernel, out_shape=(shp, shp, shp), name="rope_tables")()


def _in_proj_kernel(x_ref, w_ref, cos_ref, sin_lo_ref, sin_hi_ref, o_ref, wbf_ref):
    p = pl.program_id(0)

    @pl.when(pl.program_id(1) == 0)
    def _():
        wbf_ref[...] = w_ref[...].astype(_BF16)

    acc = jnp.dot(x_ref[...], wbf_ref[...], preferred_element_type=_F32)

    @pl.when((p < P_GMLP) | (p >= P_V))
    def _():
        o_ref[...] = acc.astype(o_ref.dtype)

    @pl.when((p >= P_GMLP) & (p < P_GATES))
    def _():
        o_ref[...] = _gelu(acc).astype(o_ref.dtype)

    @pl.when((p >= P_GATES) & (p < P_QKV))
    def _():
        o_ref[...] = jax.nn.sigmoid(acc).astype(o_ref.dtype)

    @pl.when((p >= P_QKV) & (p < P_V))
    def _():
        c = cos_ref[...]
        s_lo = sin_lo_ref[...]
        s_hi = sin_hi_ref[...]
        for h in range(PROJ_TN // HEAD_DIM):
            a = acc[:, h * HEAD_DIM:(h + 1) * HEAD_DIM]
            r = (a * c + pltpu.roll(a, HEAD_DIM - ROPE_HALF, 1) * s_lo
                 + pltpu.roll(a, ROPE_HALF, 1) * s_hi)
            o_ref[:, h * HEAD_DIM:(h + 1) * HEAD_DIM] = r.astype(o_ref.dtype)


def _in_proj(xb, w_in, layer, tables):
    seq_tiles = SEQ // PROJ_TM
    tab_spec = pl.BlockSpec((PROJ_TM, LANES), lambda p, m: (m % seq_tiles, 0))
    return pl.pallas_call(
        _in_proj_kernel,
        grid=(N_PROJ_TILES, TOKENS // PROJ_TM),
        in_specs=[
            pl.BlockSpec((PROJ_TM, D_MODEL), lambda p, m: (m, 0)),
            pl.BlockSpec((None, D_MODEL, PROJ_TN),
                         lambda p, m: (layer, 0, (p + QKV_TILES) % N_PROJ_TILES)),
            tab_spec, tab_spec, tab_spec,
        ],
        out_specs=pl.BlockSpec((PROJ_TM, PROJ_TN), lambda p, m: (m, p)),
        out_shape=jax.ShapeDtypeStruct((TOKENS, IN_WIDTH), _BF16),
        scratch_shapes=[pltpu.VMEM((D_MODEL, PROJ_TN), _BF16)],
        compiler_params=_params(("arbitrary", "arbitrary")),
        name="in_proj",
    )(xb, w_in, *tables)


def _attn_kernel(*refs):
    qkv = refs[:3 * N_GROUPS]
    o_ref = refs[3 * N_GROUPS]
    scale = HEAD_DIM ** -0.5
    for qb in range(SEQ // ATTN_TQ):
        q_lo = qb * ATTN_TQ
        parts = []
        for g, (window, dilation) in enumerate(DILATED_PATTERNS):
            q_ref, k_ref, v_ref = qkv[3 * g:3 * g + 3]
            k_lo = max(0, q_lo - window)
            span = q_lo + ATTN_TQ - k_lo
            q = q_ref[q_lo:q_lo + ATTN_TQ, :]
            k = k_ref[k_lo:k_lo + span, :]
            v = v_ref[k_lo:k_lo + span, :]
            s = lax.dot_general(q, k, (((1,), (1,)), ((), ())),
                                preferred_element_type=_F32) * scale
            ti = lax.broadcasted_iota(jnp.int32, (ATTN_TQ, span), 0)
            sj = lax.broadcasted_iota(jnp.int32, (ATTN_TQ, span), 1)
            d = ti - sj + (q_lo - k_lo)
            valid = (d >= 0) & (d <= window) & ((d & (dilation - 1)) == 0)
            s = jnp.where(valid, s, NEG_INF)
            m = jnp.max(s, axis=-1, keepdims=True)
            e = jnp.exp(s - m)
            den = jnp.sum(e, axis=-1, keepdims=True)
            pv = jnp.dot(e.astype(_BF16), v, preferred_element_type=_F32)
            parts.append((m, den, pv))
        m_all = functools.reduce(jnp.maximum, [m for m, _, _ in parts])
        num = jnp.zeros((ATTN_TQ, HEAD_DIM), _F32)
        z = jnp.zeros((ATTN_TQ, 1), _F32)
        for m, den, pv in parts:
            w = jnp.exp(m - m_all)
            num = num + w * pv
            z = z + w * den
        o_ref[q_lo:q_lo + ATTN_TQ, :] = (num / z).astype(o_ref.dtype)


def _attention(proj):
    q_blk = P_QKV * PROJ_TN // HEAD_DIM
    k_blk = q_blk + N_ATTN_HEADS
    v_blk = k_blk + N_ATTN_HEADS
    in_specs = []
    for g in range(N_GROUPS):
        for base in (q_blk, k_blk, v_blk):
            in_specs.append(pl.BlockSpec(
                (SEQ, HEAD_DIM), lambda b, s, base=base, g=g: (b, base + g * ATTN_SLOTS + s)))
    return pl.pallas_call(
        _attn_kernel,
        grid=(BATCH, ATTN_SLOTS),
        in_specs=in_specs,
        out_specs=pl.BlockSpec((SEQ, HEAD_DIM), lambda b, s: (b, s)),
        out_shape=jax.ShapeDtypeStruct((TOKENS, ATTN_OUT), _BF16),
        compiler_params=_params(("arbitrary", "arbitrary")),
        name="dilated_attention",
    )(*([proj] * (3 * N_GROUPS)))


def _conv_kernel(val_ref, gate_ref, pval_ref, pgate_ref, w_ref, b_ref, g_ref, beta_ref,
                 o_ref, hp_ref, cv_ref):
    first = pl.program_id(1) == 0
    halo = pval_ref[...].astype(_F32) * jax.nn.sigmoid(pgate_ref[...].astype(_F32))
    hp_ref[0:CONV_HALO, :] = jnp.where(first, 0.0, halo)
    hp_ref[CONV_HALO:, :] = val_ref[...].astype(_F32) * jax.nn.sigmoid(gate_ref[...].astype(_F32))
    off = CONV_HALO - (CONV_KERNEL - 1)
    for c in range(CONV_WIDTH // LANES):
        cols = slice(c * LANES, (c + 1) * LANES)
        for r in range(CONV_TS // CONV_ROWS):
            acc = jnp.zeros((CONV_ROWS, LANES), _F32)
            for j in range(CONV_KERNEL):
                acc = acc + w_ref[j:j + 1, cols] * hp_ref[pl.ds(r * CONV_ROWS + off + j, CONV_ROWS), cols]
            cv_ref[r * CONV_ROWS:(r + 1) * CONV_ROWS, cols] = acc + b_ref[:, cols]
    y = _layer_norm(cv_ref[...], g_ref[...], beta_ref[...])
    o_ref[...] = (y * jax.nn.sigmoid(y)).astype(o_ref.dtype)


def _conformer_conv(proj, w_dw, b_dw, ln_g, ln_b, layer):
    tiles = SEQ // CONV_TS
    halo_per_tile = CONV_TS // CONV_HALO
    val_blk = P_CONV * PROJ_TN // CONV_WIDTH

    def cur(col):
        return pl.BlockSpec((CONV_TS, CONV_WIDTH), lambda b, i: (b * tiles + i, col))

    def prev(col):
        return pl.BlockSpec(
            (CONV_HALO, CONV_WIDTH),
            lambda b, i: (jnp.maximum((b * tiles + i) * halo_per_tile - 1, 0), col))

    vec = pl.BlockSpec((None, 1, CONV_WIDTH), lambda b, i: (layer, 0, 0))
    return pl.pallas_call(
        _conv_kernel,
        grid=(BATCH, tiles),
        in_specs=[cur(val_blk), cur(val_blk + 1), prev(val_blk), prev(val_blk + 1),
                  pl.BlockSpec((None, CONV_KERNEL, CONV_WIDTH), lambda b, i: (layer, 0, 0)),
                  vec, vec, vec],
        out_specs=pl.BlockSpec((CONV_TS, CONV_WIDTH), lambda b, i: (b * tiles + i, 0)),
        out_shape=jax.ShapeDtypeStruct((TOKENS, CONV_WIDTH), _BF16),
        scratch_shapes=[pltpu.VMEM((CONV_HALO + CONV_TS, CONV_WIDTH), _F32),
                        pltpu.VMEM((CONV_TS, CONV_WIDTH), _F32)],
        compiler_params=_params(("arbitrary", "arbitrary")),
        name="conformer_conv",
    )(proj, proj, proj, proj, w_dw,
      b_dw.reshape(DEPTH, 1, CONV_WIDTH), ln_g.reshape(DEPTH, 1, CONV_WIDTH),
      ln_b.reshape(DEPTH, 1, CONV_WIDTH))


def _sgu_kernel(z_ref, ws_ref, bs_ref, g_ref, b_ref, o_ref):
    chunks = SGU_TM // GMLP_CHUNK
    vn = _layer_norm(z_ref[:, GMLP_WIDTH:].astype(_F32), g_ref[...], b_ref[...]).astype(_BF16)
    row = lax.broadcasted_iota(jnp.int32, (GMLP_CHUNK, GMLP_CHUNK), 0)
    col = lax.broadcasted_iota(jnp.int32, (GMLP_CHUNK, GMLP_CHUNK), 1)
    causal = col <= row
    for g in range(GMLP_GROUPS):
        cols = slice(g * GMLP_GROUP_CH, (g + 1) * GMLP_GROUP_CH)
        wm = jnp.where(causal, ws_ref[g], 0.0).astype(_BF16)
        rhs = jnp.concatenate(
            [vn[n * GMLP_CHUNK:(n + 1) * GMLP_CHUNK, cols] for n in range(chunks)], axis=1)
        s = jnp.dot(wm, rhs, preferred_element_type=_F32) + bs_ref[:, g:g + 1]
        for n in range(chunks):
            rows = slice(n * GMLP_CHUNK, (n + 1) * GMLP_CHUNK)
            u = z_ref[rows, cols].astype(_F32)
            o_ref[rows, cols] = (u * s[:, n * GMLP_GROUP_CH:(n + 1) * GMLP_GROUP_CH]).astype(o_ref.dtype)


def _spatial_gating(proj, w_spatial, b_spatial_t, ln_g, ln_b, layer):
    z_blk = P_GMLP * PROJ_TN // (2 * GMLP_WIDTH)
    vec = pl.BlockSpec((None, 1, GMLP_WIDTH), lambda i: (layer, 0, 0))
    return pl.pallas_call(
        _sgu_kernel,
        grid=(TOKENS // SGU_TM,),
        in_specs=[pl.BlockSpec((SGU_TM, 2 * GMLP_WIDTH), lambda i: (i, z_blk)),
                  pl.BlockSpec((None, GMLP_GROUPS, GMLP_CHUNK, GMLP_CHUNK), lambda i: (layer, 0, 0, 0)),
                  pl.BlockSpec((None, GMLP_CHUNK, GMLP_GROUPS), lambda i: (layer, 0, 0)),
                  vec, vec],
        out_specs=pl.BlockSpec((SGU_TM, GMLP_WIDTH), lambda i: (i, 0)),
        out_shape=jax.ShapeDtypeStruct((TOKENS, GMLP_WIDTH), _BF16),
        compiler_params=_params(("arbitrary",)),
        name="spatial_gating",
    )(proj, w_spatial, b_spatial_t, ln_g.reshape(DEPTH, 1, GMLP_WIDTH),
      ln_b.reshape(DEPTH, 1, GMLP_WIDTH))


def _merge_kernel(attn_ref, hc_ref, hg_ref, ga_ref, gc_ref, gm_ref, x_ref,
                  wa_ref, wc_ref, wg_ref, wo_ref, g_ref, b_ref, xo_ref, xob_ref):
    ya = jnp.dot(attn_ref[...], wa_ref[...], preferred_element_type=_F32)
    merged = ga_ref[...].astype(_F32) * ya
    yc = jnp.dot(hc_ref[...], wc_ref[...], preferred_element_type=_F32)
    merged = merged + gc_ref[...].astype(_F32) * yc
    yg = jnp.dot(hg_ref[...], wg_ref[...], preferred_element_type=_F32)
    merged = merged + gm_ref[...].astype(_F32) * yg
    y = jnp.dot(merged.astype(_BF16), wo_ref[...], preferred_element_type=_F32)
    out = _layer_norm(DEEPNORM_ALPHA * x_ref[...] + y, g_ref[...], b_ref[...])
    xo_ref[...] = out
    xob_ref[...] = out.astype(_BF16)


def _resident(shape):
    return pl.BlockSpec(shape, lambda i: (0,) * len(shape), pipeline_mode=pl.Buffered(1))


def _merge(attn, hconv, hgmlp, proj, x, wa, wc, wg, wo, ln_g, ln_b, layer):
    gate_blk = P_GATES * PROJ_TN // D_MODEL
    tm = MERGE_TM
    row = lambda width: pl.BlockSpec((tm, width), lambda i: (i, 0))
    gate = lambda k: pl.BlockSpec((tm, D_MODEL), lambda i: (i, gate_blk + k))
    vec = pl.BlockSpec((None, 1, D_MODEL), lambda i: (layer, 0, 0))
    out_spec = pl.BlockSpec((tm, D_MODEL), lambda i: (i, 0))
    return pl.pallas_call(
        _merge_kernel,
        grid=(TOKENS // tm,),
        in_specs=[row(ATTN_OUT), row(CONV_WIDTH), row(GMLP_WIDTH), gate(0), gate(1), gate(2),
                  row(D_MODEL),
                  _resident((ATTN_OUT, D_MODEL)), _resident((CONV_WIDTH, D_MODEL)),
                  _resident((GMLP_WIDTH, D_MODEL)), _resident((D_MODEL, D_MODEL)),
                  vec, vec],
        out_specs=(out_spec, out_spec),
        out_shape=(jax.ShapeDtypeStruct((TOKENS, D_MODEL), _F32),
                   jax.ShapeDtypeStruct((TOKENS, D_MODEL), _BF16)),
        compiler_params=_params(("arbitrary",)),
        name="merge_out_ln1",
    )(attn, hconv, hgmlp, proj, proj, proj, x, wa, wc, wg, wo,
      ln_g.reshape(DEPTH, 1, D_MODEL), ln_b.reshape(DEPTH, 1, D_MODEL))


def _ffn_up_kernel(x_ref, wg_ref, wu_ref, o_ref, wgb_ref, wub_ref):
    @pl.when(pl.program_id(1) == 0)
    def _():
        wgb_ref[...] = wg_ref[...].astype(_BF16)
        wub_ref[...] = wu_ref[...].astype(_BF16)

    x = x_ref[...]
    a = jnp.dot(x, wgb_ref[...], preferred_element_type=_F32)
    u = jnp.dot(x, wub_ref[...], preferred_element_type=_F32)
    o_ref[...] = (a * jax.nn.sigmoid(a) * u).astype(o_ref.dtype)


def _ffn_up(xb, w_gate, w_up, layer):
    w_spec = pl.BlockSpec((None, D_MODEL, FFN_TN), lambda n, m: (layer, 0, n))
    return pl.pallas_call(
        _ffn_up_kernel,
        grid=(FFN_HIDDEN // FFN_TN, TOKENS // PROJ_TM),
        in_specs=[pl.BlockSpec((PROJ_TM, D_MODEL), lambda n, m: (m, 0)), w_spec, w_spec],
        out_specs=pl.BlockSpec((PROJ_TM, FFN_TN), lambda n, m: (m, n)),
        out_shape=jax.ShapeDtypeStruct((TOKENS, FFN_HIDDEN), _BF16),
        scratch_shapes=[pltpu.VMEM((D_MODEL, FFN_TN), _BF16), pltpu.VMEM((D_MODEL, FFN_TN), _BF16)],
        compiler_params=_params(("arbitrary", "arbitrary")),
        name="ffn_up",
    )(xb, w_gate, w_up)


def _ffn_down_kernel(h_ref, x_ref, w_ref, g_ref, b_ref, xo_ref, xob_ref):
    f = jnp.dot(h_ref[...], w_ref[...], preferred_element_type=_F32)
    out = _layer_norm(DEEPNORM_ALPHA * x_ref[...] + f, g_ref[...], b_ref[...])
    xo_ref[...] = out
    xob_ref[...] = out.astype(_BF16)


def _ffn_down(h, x, w_down, ln_g, ln_b, layer):
    tm = DOWN_TM
    vec = pl.BlockSpec((None, 1, D_MODEL), lambda i: (layer, 0, 0))
    out_spec = pl.BlockSpec((tm, D_MODEL), lambda i: (i, 0))
    return pl.pallas_call(
        _ffn_down_kernel,
        grid=(TOKENS // tm,),
        in_specs=[pl.BlockSpec((tm, FFN_HIDDEN), lambda i: (i, 0)),
                  pl.BlockSpec((tm, D_MODEL), lambda i: (i, 0)),
                  _resident((FFN_HIDDEN, D_MODEL)), vec, vec],
        out_specs=(out_spec, out_spec),
        out_shape=(jax.ShapeDtypeStruct((TOKENS, D_MODEL), _F32),
                   jax.ShapeDtypeStruct((TOKENS, D_MODEL), _BF16)),
        compiler_params=_params(("arbitrary",)),
        name="ffn_down_ln2",
    )(h, x, w_down, ln_g.reshape(DEPTH, 1, D_MODEL), ln_b.reshape(DEPTH, 1, D_MODEL))


def kernel(x, w_in, w_attn_proj, conv_dw, conv_dw_b, conv_ln_g, conv_ln_b, w_conv_proj, gmlp_ln_g, gmlp_ln_b, w_spatial, b_spatial, w_gmlp_proj, w_out, ln1_g, ln1_b, w_ffn_gate, w_ffn_up, w_ffn_down, ln2_g, ln2_b):
    assert x.shape == (BATCH, SEQ, D_MODEL) and w_in.shape == (DEPTH, D_MODEL, IN_WIDTH)
    tables = _rope_tables()
    xf = x.reshape(TOKENS, D_MODEL)
    xb = xf.astype(_BF16)
    b_spatial_t = jnp.swapaxes(b_spatial, 1, 2)
    for layer in range(DEPTH):
        proj = _in_proj(xb, w_in, layer, tables)
        attn = _attention(proj)
        hconv = _conformer_conv(proj, conv_dw, conv_dw_b, conv_ln_g, conv_ln_b, layer)
        hgmlp = _spatial_gating(proj, w_spatial, b_spatial_t, gmlp_ln_g, gmlp_ln_b, layer)
        xf, xb = _merge(attn, hconv, hgmlp, proj, xf,
                        w_attn_proj[layer].astype(_BF16), w_conv_proj[layer].astype(_BF16),
                        w_gmlp_proj[layer].astype(_BF16), w_out[layer].astype(_BF16),
                        ln1_g, ln1_b, layer)
        h = _ffn_up(xb, w_ffn_gate, w_ffn_up, layer)
        xf, xb = _ffn_down(h, xf, w_ffn_down[layer].astype(_BF16), ln2_g, ln2_b, layer)
    return xf.reshape(BATCH, SEQ, D_MODEL)
```

```python
import functools
import math

import jax
import jax.numpy as jnp
from jax import lax
from jax.experimental import pallas as pl
from jax.experimental.pallas import tpu as pltpu

D_MODEL = 2048
BATCH = 4
SEQ = 2048
DEPTH = 2
TOKENS = BATCH * SEQ
HEAD_DIM = 128
ATTN_SLOTS = 4
DILATED_PATTERNS = ((128, 1), (512, 4), (2048, 16))
N_GROUPS = len(DILATED_PATTERNS)
N_ATTN_HEADS = N_GROUPS * ATTN_SLOTS
ATTN_WIDTH = N_ATTN_HEADS * HEAD_DIM
ATTN_OUT = ATTN_SLOTS * HEAD_DIM
ATTN_BAND = 128
ROPE_THETA = 500000.0
ROPE_DIM = HEAD_DIM // 4
ROPE_HALF = ROPE_DIM // 2
CONV_WIDTH = D_MODEL // 2
CONV_KERNEL = 31
GMLP_WIDTH = D_MODEL // 2
GMLP_CHUNK = 128
GMLP_GROUPS = 8
GMLP_GROUP_CH = GMLP_WIDTH // GMLP_GROUPS
FFN_HIDDEN = -(-8 * D_MODEL // (3 * 256)) * 256
DEEPNORM_ALPHA = (2 * DEPTH) ** 0.25
LN_EPS = 1e-5
NEG_INF = -1e30
IN_WIDTH = 3 * ATTN_WIDTH + 2 * CONV_WIDTH + 2 * GMLP_WIDTH + 3 * D_MODEL

LANES = 128
SUBLANES = 8
V7X_VMEM_LIMIT_BYTES = 56 * 1024 * 1024

W_TILE = 512
PROJ_TM = 1024
T_Q, T_K, T_V = 0, ATTN_WIDTH // W_TILE, 2 * ATTN_WIDTH // W_TILE
T_CONV = 3 * ATTN_WIDTH // W_TILE
T_GMLP = T_CONV + 2 * CONV_WIDTH // W_TILE
T_GATES = T_GMLP + 2 * GMLP_WIDTH // W_TILE
ATTN_TQ = 128
CONV_TS = 256
CONV_HALO = 32
CONV_ROWS = 128
SGU_TM = 512
MERGE_TM = 256
DOWN_TM = 256

_F32 = jnp.float32
_BF16 = jnp.bfloat16


def _params(semantics):
    return pltpu.CompilerParams(dimension_semantics=semantics,
                                vmem_limit_bytes=V7X_VMEM_LIMIT_BYTES)


def _layer_norm(x, g, b):
    mu = jnp.mean(x, axis=-1, keepdims=True)
    xc = x - mu
    var = jnp.mean(xc * xc, axis=-1, keepdims=True)
    return xc * lax.rsqrt(var + LN_EPS) * g + b


def _gelu(x):
    return 0.5 * x * (1.0 + lax.erf(x * math.sqrt(0.5)))


def _rope_table_kernel(cos_ref, sin_lo_ref, sin_hi_ref):
    pos = lax.broadcasted_iota(jnp.int32, (SEQ, LANES), 0).astype(_F32)
    lane = lax.broadcasted_iota(jnp.int32, (SEQ, LANES), 1)
    fidx = (lane & (ROPE_HALF - 1)).astype(_F32)
    inv_freq = jnp.exp(fidx * (-math.log(ROPE_THETA) / ROPE_HALF))
    ang = pos * inv_freq
    c = jnp.cos(ang)
    s = jnp.sin(ang)
    cos_ref[...] = jnp.where(lane < ROPE_DIM, c, 1.0)
    sin_lo_ref[...] = jnp.where(lane < ROPE_HALF, -s, 0.0)
    sin_hi_ref[...] = jnp.where((lane >= ROPE_HALF) & (lane < ROPE_DIM), s, 0.0)


def _rope_tables():
    shp = jax.ShapeDtypeStruct((SEQ, LANES), _F32)
    return pl.pallas_call(_rope_table_kernel, out_shape=(shp, shp, shp), name="rope_tables")()


def _proj_kernel(*refs, n_w, epilogue):
    x_ref = refs[0]
    w_refs = refs[1:1 + n_w]
    extra = refs[1 + n_w:-2]
    o_ref, wbf_ref = refs[-2:]

    @pl.when(pl.program_id(1) == 0)
    def _():
        for k, w_ref in enumerate(w_refs):
            wbf_ref[:, k * W_TILE:(k + 1) * W_TILE] = w_ref[...].astype(_BF16)

    acc = jnp.dot(x_ref[...], wbf_ref[...], preferred_element_type=_F32)
    if epilogue == "plain":
        out = acc
    elif epilogue == "gelu":
        out = _gelu(acc)
    elif epilogue == "sigmoid":
        out = jax.nn.sigmoid(acc)
    elif epilogue == "glu":
        out = acc[:, :W_TILE] * jax.nn.sigmoid(acc[:, W_TILE:])
    elif epilogue == "swiglu":
        a = acc[:, :W_TILE]
        out = a * jax.nn.sigmoid(a) * acc[:, W_TILE:]
    elif epilogue == "rope_qk":
        cos_ref, sin_lo_ref, sin_hi_ref = extra
        c, s_lo, s_hi = cos_ref[...], sin_lo_ref[...], sin_hi_ref[...]
        heads = n_w * W_TILE // HEAD_DIM
        first_head = pl.program_id(0) * heads
        for h in range(heads):
            a = acc[:, h * HEAD_DIM:(h + 1) * HEAD_DIM]
            r = (a * c + pltpu.roll(a, HEAD_DIM - ROPE_HALF, 1) * s_lo
                 + pltpu.roll(a, ROPE_HALF, 1) * s_hi)
            r = r * jnp.where(first_head + h < N_ATTN_HEADS, HEAD_DIM ** -0.5, 1.0)
            o_ref[:, h * HEAD_DIM:(h + 1) * HEAD_DIM] = r.astype(o_ref.dtype)
        return
    else:
        raise ValueError(epilogue)
    o_ref[...] = out.astype(o_ref.dtype)


def _project(xb, weights, n_steps, epilogue, name, tables=()):
    n_w = len(weights)
    out_tn = W_TILE if epilogue in ("glu", "swiglu") else n_w * W_TILE
    seq_tiles = SEQ // PROJ_TM
    w_specs = [pl.BlockSpec((None, D_MODEL, W_TILE),
                            lambda j, m, layer=layer, first=first, stride=stride:
                            (layer, 0, first + stride * j))
               for _, layer, first, stride in weights]
    tab_specs = [pl.BlockSpec((PROJ_TM, LANES), lambda j, m: (m % seq_tiles, 0)) for _ in tables]
    return pl.pallas_call(
        functools.partial(_proj_kernel, n_w=n_w, epilogue=epilogue),
        grid=(n_steps, TOKENS // PROJ_TM),
        in_specs=[pl.BlockSpec((PROJ_TM, D_MODEL), lambda j, m: (m, 0))] + w_specs + tab_specs,
        out_specs=pl.BlockSpec((PROJ_TM, out_tn), lambda j, m: (m, j)),
        out_shape=jax.ShapeDtypeStruct((TOKENS, n_steps * out_tn), _BF16),
        scratch_shapes=[pltpu.VMEM((D_MODEL, n_w * W_TILE), _BF16)],
        compiler_params=_params(("arbitrary", "arbitrary")),
        name=name,
    )(xb, *[w for w, _, _, _ in weights], *tables)


def _attn_kernel(q0, k0, v0, q1, k1, v1, q2, k2, v2, o_ref, *scratch):
    groups = ((q0, k0, v0), (q1, k1, v1), (q2, k2, v2))
    stats = scratch[:3 * N_GROUPS]
    vcs = scratch[3 * N_GROUPS:4 * N_GROUPS]
    regroup = scratch[4 * N_GROUPS:]
    qi = lax.broadcasted_iota(jnp.int32, (ATTN_TQ, 2 * ATTN_BAND), 0)
    kj = lax.broadcasted_iota(jnp.int32, (ATTN_TQ, 2 * ATTN_BAND), 1)
    band_mask = (kj >= qi) & (kj <= qi + ATTN_BAND)
    causal_mask = (lax.broadcasted_iota(jnp.int32, (ATTN_TQ, ATTN_BAND), 1)
                   <= lax.broadcasted_iota(jnp.int32, (ATTN_TQ, ATTN_BAND), 0))

    for g, (window, dil) in enumerate(DILATED_PATTERNS):
        q_ref, k_ref, v_ref = groups[g]
        m_ref, den_ref, pv_ref = stats[3 * g:3 * g + 3]
        vc_ref = vcs[g]
        vc_ref[:, HEAD_DIM:] = jnp.ones((SEQ, HEAD_DIM), _BF16)
        length = SEQ // dil
        if dil == 1:
            q_src, k_src = q_ref, k_ref
            vc_ref[:, :HEAD_DIM] = v_ref[...]
        else:
            q_stage, q_src, k_stage, k_src, v_stage = regroup[5 * (g - 1):5 * g]
            for src, stage_ref, dst in ((q_ref, q_stage, q_src), (k_ref, k_stage, k_src),
                                        (v_ref, v_stage, vc_ref)):
                stage_ref[...] = src[...].astype(_F32)
                for c in range(dil):
                    rows = stage_ref[pl.ds(c, length, stride=dil), :].astype(_BF16)
                    dst[c * length:(c + 1) * length, :HEAD_DIM] = rows
        for c in range(dil):
            for i in range(length // ATTN_TQ):
                row0 = c * length + i * ATTN_TQ
                k_lo = row0 if i == 0 else row0 - ATTN_BAND
                q = q_src[row0:row0 + ATTN_TQ, :]
                k = k_src[k_lo:row0 + ATTN_TQ, :]
                s = lax.dot_general(q, k, (((1,), (1,)), ((), ())), preferred_element_type=_F32)
                s = jnp.where(causal_mask if i == 0 else band_mask, s, NEG_INF)
                m = jnp.max(s, axis=-1, keepdims=True)
                e = jnp.exp(s - m).astype(_BF16)
                r = jnp.dot(e, vc_ref[k_lo:row0 + ATTN_TQ, :], preferred_element_type=_F32)
                if dil == 1:
                    tok = pl.ds(row0, ATTN_TQ)
                else:
                    tok = pl.ds(c + dil * i * ATTN_TQ, ATTN_TQ, stride=dil)
                m_ref[tok, :] = jnp.broadcast_to(m, (ATTN_TQ, HEAD_DIM))
                den_ref[tok, :] = r[:, HEAD_DIM:]
                pv_ref[tok, :] = r[:, :HEAD_DIM]

    rows_per_step = 2 * ATTN_TQ
    for t in range(SEQ // rows_per_step):
        rows = slice(t * rows_per_step, (t + 1) * rows_per_step)
        ms = [stats[3 * g][rows, :] for g in range(N_GROUPS)]
        m_all = functools.reduce(jnp.maximum, ms)
        num = jnp.zeros((rows_per_step, HEAD_DIM), _F32)
        z = jnp.zeros((rows_per_step, HEAD_DIM), _F32)
        for g in range(N_GROUPS):
            w = jnp.exp(ms[g] - m_all)
            z = z + w * stats[3 * g + 1][rows, :]
            num = num + w * stats[3 * g + 2][rows, :]
        o_ref[rows, :] = (num / z).astype(o_ref.dtype)


def _attention(qk, v):
    in_specs = []
    for g in range(N_GROUPS):
        for base in (0, N_ATTN_HEADS, 0):
            in_specs.append(pl.BlockSpec(
                (SEQ, HEAD_DIM), lambda b, s, base=base, g=g: (b, base + g * ATTN_SLOTS + s)))
    tok_f32 = pltpu.VMEM((SEQ, HEAD_DIM), _F32)
    tok_bf16 = pltpu.VMEM((SEQ, HEAD_DIM), _BF16)
    dilated_groups = sum(1 for _, dil in DILATED_PATTERNS if dil > 1)
    return pl.pallas_call(
        _attn_kernel,
        grid=(BATCH, ATTN_SLOTS),
        in_specs=in_specs,
        out_specs=pl.BlockSpec((SEQ, HEAD_DIM), lambda b, s: (b, s)),
        out_shape=jax.ShapeDtypeStruct((TOKENS, ATTN_OUT), _BF16),
        scratch_shapes=([tok_f32] * (3 * N_GROUPS)
                        + [pltpu.VMEM((SEQ, 2 * HEAD_DIM), _BF16)] * N_GROUPS
                        + [tok_f32, tok_bf16, tok_f32, tok_bf16, tok_f32] * dilated_groups),
        compiler_params=_params(("arbitrary", "arbitrary")),
        name="dilated_attention",
    )(*([qk, qk, v] * N_GROUPS))


def _conv_kernel(h_ref, ph_ref, w_ref, b_ref, g_ref, beta_ref, o_ref, hp_ref, cv_ref):
    first = pl.program_id(1) == 0
    n_chunks = CONV_WIDTH // LANES
    for c in range(n_chunks):
        cols = slice(c * LANES, (c + 1) * LANES)
        hp_ref[0, c, 0:CONV_HALO, :] = jnp.where(first, 0.0, ph_ref[:, cols].astype(_F32))
        hp_ref[0, c, CONV_HALO:, :] = h_ref[:, cols].astype(_F32)
    off = CONV_HALO - (CONV_KERNEL - 1)
    shifted_rows = CONV_HALO + CONV_TS - SUBLANES

    def realign(c, carry):
        for shift in range(1, SUBLANES):
            hp_ref[shift, c, 0:shifted_rows, :] = hp_ref[0, c, pl.ds(shift, shifted_rows), :]
        return carry

    lax.fori_loop(0, n_chunks, realign, 0)

    vregs_per_block = CONV_ROWS // SUBLANES
    blocks_per_chunk = CONV_TS // CONV_ROWS
    taps_by_shift = [[j for j in range(CONV_KERNEL) if (off + j) % SUBLANES == shift]
                     for shift in range(SUBLANES)]

    def conv_block(i, carry):
        c = i // blocks_per_chunk
        base = pl.multiple_of((i % blocks_per_chunk) * CONV_ROWS, CONV_ROWS)
        accs = [jnp.broadcast_to(b_ref[c], (SUBLANES, LANES))] * vregs_per_block
        for shift, taps in enumerate(taps_by_shift):
            reach = max((off + j) // SUBLANES for j in taps) + vregs_per_block
            win = [hp_ref[shift, c, pl.ds(base + u * SUBLANES, SUBLANES), :] for u in range(reach)]
            for j in taps:
                a = (off + j) // SUBLANES
                wj = jnp.broadcast_to(w_ref[c, j:j + 1, :], (SUBLANES, LANES))
                for v in range(vregs_per_block):
                    accs[v] = accs[v] + wj * win[v + a]
        for v in range(vregs_per_block):
            cv_ref[c, pl.ds(base + v * SUBLANES, SUBLANES), :] = accs[v]
        return carry

    lax.fori_loop(0, n_chunks * blocks_per_chunk, conv_block, 0)
    conv = jnp.concatenate([cv_ref[c] for c in range(n_chunks)], axis=1)
    y = _layer_norm(conv, g_ref[...], beta_ref[...])
    o_ref[...] = (y * jax.nn.sigmoid(y)).astype(o_ref.dtype)


def _conformer_conv(h, w_dw, b_dw, ln_g, ln_b, layer):
    tiles = SEQ // CONV_TS
    halo_per_tile = CONV_TS // CONV_HALO
    n_chunks = CONV_WIDTH // LANES
    vec = pl.BlockSpec((None, 1, CONV_WIDTH), lambda b, i: (layer, 0, 0))
    w_chunks = jnp.swapaxes(w_dw.reshape(DEPTH, CONV_KERNEL, n_chunks, LANES), 1, 2)
    b_chunks = b_dw.reshape(DEPTH, n_chunks, 1, LANES)
    return pl.pallas_call(
        _conv_kernel,
        grid=(BATCH, tiles),
        in_specs=[pl.BlockSpec((CONV_TS, CONV_WIDTH), lambda b, i: (b * tiles + i, 0)),
                  pl.BlockSpec((CONV_HALO, CONV_WIDTH),
                               lambda b, i: (jnp.maximum((b * tiles + i) * halo_per_tile - 1, 0), 0)),
                  pl.BlockSpec((None, n_chunks, CONV_KERNEL, LANES), lambda b, i: (layer, 0, 0, 0)),
                  pl.BlockSpec((None, n_chunks, 1, LANES), lambda b, i: (layer, 0, 0, 0)),
                  vec, vec],
        out_specs=pl.BlockSpec((CONV_TS, CONV_WIDTH), lambda b, i: (b * tiles + i, 0)),
        out_shape=jax.ShapeDtypeStruct((TOKENS, CONV_WIDTH), _BF16),
        scratch_shapes=[pltpu.VMEM((SUBLANES, n_chunks, CONV_HALO + CONV_TS, LANES), _F32),
                        pltpu.VMEM((n_chunks, CONV_TS, LANES), _F32)],
        compiler_params=_params(("arbitrary", "arbitrary")),
        name="conformer_conv",
    )(h, h, w_chunks, b_chunks, ln_g.reshape(DEPTH, 1, CONV_WIDTH),
      ln_b.reshape(DEPTH, 1, CONV_WIDTH))


def _sgu_kernel(z_ref, ws_ref, bs_ref, g_ref, b_ref, o_ref):
    chunks = SGU_TM // GMLP_CHUNK
    vn = _layer_norm(z_ref[:, GMLP_WIDTH:].astype(_F32), g_ref[...], b_ref[...]).astype(_BF16)
    row = lax.broadcasted_iota(jnp.int32, (GMLP_CHUNK, GMLP_CHUNK), 0)
    col = lax.broadcasted_iota(jnp.int32, (GMLP_CHUNK, GMLP_CHUNK), 1)
    causal = col <= row
    for g in range(GMLP_GROUPS):
        cols = slice(g * GMLP_GROUP_CH, (g + 1) * GMLP_GROUP_CH)
        wm = jnp.where(causal, ws_ref[g], 0.0).astype(_BF16)
        rhs = jnp.concatenate(
            [vn[n * GMLP_CHUNK:(n + 1) * GMLP_CHUNK, cols] for n in range(chunks)], axis=1)
        s = jnp.dot(wm, rhs, preferred_element_type=_F32) + bs_ref[:, g:g + 1]
        for n in range(chunks):
            rows = slice(n * GMLP_CHUNK, (n + 1) * GMLP_CHUNK)
            u = z_ref[rows, cols].astype(_F32)
            o_ref[rows, cols] = (u * s[:, n * GMLP_GROUP_CH:(n + 1) * GMLP_GROUP_CH]).astype(o_ref.dtype)


def _spatial_gating(z, w_spatial, b_spatial_t, ln_g, ln_b, layer):
    vec = pl.BlockSpec((None, 1, GMLP_WIDTH), lambda i: (layer, 0, 0))
    return pl.pallas_call(
        _sgu_kernel,
        grid=(TOKENS // SGU_TM,),
        in_specs=[pl.BlockSpec((SGU_TM, 2 * GMLP_WIDTH), lambda i: (i, 0)),
                  pl.BlockSpec((None, GMLP_GROUPS, GMLP_CHUNK, GMLP_CHUNK), lambda i: (layer, 0, 0, 0)),
                  pl.BlockSpec((None, GMLP_CHUNK, GMLP_GROUPS), lambda i: (layer, 0, 0)),
                  vec, vec],
        out_specs=pl.BlockSpec((SGU_TM, GMLP_WIDTH), lambda i: (i, 0)),
        out_shape=jax.ShapeDtypeStruct((TOKENS, GMLP_WIDTH), _BF16),
        compiler_params=_params(("arbitrary",)),
        name="spatial_gating",
    )(z, w_spatial, b_spatial_t, ln_g.reshape(DEPTH, 1, GMLP_WIDTH),
      ln_b.reshape(DEPTH, 1, GMLP_WIDTH))


def _merge_kernel(attn_ref, hc_ref, hg_ref, ga_ref, gc_ref, gm_ref, x_ref,
                  wa_ref, wc_ref, wg_ref, wo_ref, g_ref, b_ref, xo_ref, xob_ref):
    ya = jnp.dot(attn_ref[...], wa_ref[...], preferred_element_type=_F32)
    merged = ga_ref[...].astype(_F32) * ya
    yc = jnp.dot(hc_ref[...], wc_ref[...], preferred_element_type=_F32)
    merged = merged + gc_ref[...].astype(_F32) * yc
    yg = jnp.dot(hg_ref[...], wg_ref[...], preferred_element_type=_F32)
    merged = merged + gm_ref[...].astype(_F32) * yg
    y = jnp.dot(merged.astype(_BF16), wo_ref[...], preferred_element_type=_F32)
    out = _layer_norm(DEEPNORM_ALPHA * x_ref[...] + y, g_ref[...], b_ref[...])
    xo_ref[...] = out
    xob_ref[...] = out.astype(_BF16)


def _resident(shape):
    return pl.BlockSpec(shape, lambda i: (0,) * len(shape), pipeline_mode=pl.Buffered(1))


def _merge(attn, hconv, hgmlp, gates, x, wa, wc, wg, wo, ln_g, ln_b, layer):
    tm = MERGE_TM
    row = lambda width: pl.BlockSpec((tm, width), lambda i: (i, 0))
    gate = lambda k: pl.BlockSpec((tm, D_MODEL), lambda i: (i, k))
    vec = pl.BlockSpec((None, 1, D_MODEL), lambda i: (layer, 0, 0))
    out_spec = pl.BlockSpec((tm, D_MODEL), lambda i: (i, 0))
    return pl.pallas_call(
        _merge_kernel,
        grid=(TOKENS // tm,),
        in_specs=[row(ATTN_OUT), row(CONV_WIDTH), row(GMLP_WIDTH), gate(0), gate(1), gate(2),
                  row(D_MODEL),
                  _resident((ATTN_OUT, D_MODEL)), _resident((CONV_WIDTH, D_MODEL)),
                  _resident((GMLP_WIDTH, D_MODEL)), _resident((D_MODEL, D_MODEL)),
                  vec, vec],
        out_specs=(out_spec, out_spec),
        out_shape=(jax.ShapeDtypeStruct((TOKENS, D_MODEL), _F32),
                   jax.ShapeDtypeStruct((TOKENS, D_MODEL), _BF16)),
        compiler_params=_params(("arbitrary",)),
        name="merge_out_ln1",
    )(attn, hconv, hgmlp, gates, gates, gates, x, wa, wc, wg, wo,
      ln_g.reshape(DEPTH, 1, D_MODEL), ln_b.reshape(DEPTH, 1, D_MODEL))


def _ffn_down_kernel(h_ref, x_ref, w_ref, g_ref, b_ref, xo_ref, xob_ref):
    f = jnp.dot(h_ref[...], w_ref[...], preferred_element_type=_F32)
    out = _layer_norm(DEEPNORM_ALPHA * x_ref[...] + f, g_ref[...], b_ref[...])
    xo_ref[...] = out
    xob_ref[...] = out.astype(_BF16)


def _ffn_down(h, x, w_down, ln_g, ln_b, layer):
    tm = DOWN_TM
    vec = pl.BlockSpec((None, 1, D_MODEL), lambda i: (layer, 0, 0))
    out_spec = pl.BlockSpec((tm, D_MODEL), lambda i: (i, 0))
    return pl.pallas_call(
        _ffn_down_kernel,
        grid=(TOKENS // tm,),
        in_specs=[pl.BlockSpec((tm, FFN_HIDDEN), lambda i: (i, 0)),
                  pl.BlockSpec((tm, D_MODEL), lambda i: (i, 0)),
                  _resident((FFN_HIDDEN, D_MODEL)), vec, vec],
        out_specs=(out_spec, out_spec),
        out_shape=(jax.ShapeDtypeStruct((TOKENS, D_MODEL), _F32),
                   jax.ShapeDtypeStruct((TOKENS, D_MODEL), _BF16)),
        compiler_params=_params(("arbitrary",)),
        name="ffn_down_ln2",
    )(h, x, w_down, ln_g.reshape(DEPTH, 1, D_MODEL), ln_b.reshape(DEPTH, 1, D_MODEL))


def kernel(x, w_in, w_attn_proj, conv_dw, conv_dw_b, conv_ln_g, conv_ln_b, w_conv_proj, gmlp_ln_g, gmlp_ln_b, w_spatial, b_spatial, w_gmlp_proj, w_out, ln1_g, ln1_b, w_ffn_gate, w_ffn_up, w_ffn_down, ln2_g, ln2_b):
    assert x.shape == (BATCH, SEQ, D_MODEL) and w_in.shape == (DEPTH, D_MODEL, IN_WIDTH)
    tables = _rope_tables()
    xf = x.reshape(TOKENS, D_MODEL)
    xb = xf.astype(_BF16)
    b_spatial_t = jnp.swapaxes(b_spatial, 1, 2)
    for layer in range(DEPTH):
        qk = _project(xb, [(w_in, layer, T_Q, 2), (w_in, layer, T_Q + 1, 2)],
                      2 * ATTN_WIDTH // (2 * W_TILE), "rope_qk", "proj_qk", tables)
        v = _project(xb, [(w_in, layer, T_V, 1)], ATTN_WIDTH // W_TILE, "plain", "proj_v")
        h = _project(xb, [(w_in, layer, T_CONV, 1), (w_in, layer, T_CONV + CONV_WIDTH // W_TILE, 1)],
                     CONV_WIDTH // W_TILE, "glu", "proj_glu")
        z = _project(xb, [(w_in, layer, T_GMLP, 2), (w_in, layer, T_GMLP + 1, 2)],
                     2 * GMLP_WIDTH // (2 * W_TILE), "gelu", "proj_gelu")
        gates = _project(xb, [(w_in, layer, T_GATES, 2), (w_in, layer, T_GATES + 1, 2)],
                         3 * D_MODEL // (2 * W_TILE), "sigmoid", "proj_gates")
        attn = _attention(qk, v)
        hconv = _conformer_conv(h, conv_dw, conv_dw_b, conv_ln_g, conv_ln_b, layer)
        hgmlp = _spatial_gating(z, w_spatial, b_spatial_t, gmlp_ln_g, gmlp_ln_b, layer)
        xf, xb = _merge(attn, hconv, hgmlp, gates, xf,
                        w_attn_proj[layer].astype(_BF16), w_conv_proj[layer].astype(_BF16),
                        w_gmlp_proj[layer].astype(_BF16), w_out[layer].astype(_BF16),
                        ln1_g, ln1_b, layer)
        hid = _project(xb, [(w_ffn_gate, layer, 0, 1), (w_ffn_up, layer, 0, 1)],
                       FFN_HIDDEN // W_TILE, "swiglu", "ffn_up")
        xf, xb = _ffn_down(hid, xf, w_ffn_down[layer].astype(_BF16), ln2_g, ln2_b, layer)
    return xf.reshape(BATCH, SEQ, D_MODEL)
```

```python
import functools
import math

import jax
import jax.numpy as jnp
from jax import lax
from jax.experimental import pallas as pl
from jax.experimental.pallas import tpu as pltpu

D_MODEL = 2048
BATCH = 4
SEQ = 2048
DEPTH = 2
TOKENS = BATCH * SEQ
HEAD_DIM = 128
ATTN_SLOTS = 4
DILATED_PATTERNS = ((128, 1), (512, 4), (2048, 16))
N_GROUPS = len(DILATED_PATTERNS)
N_ATTN_HEADS = N_GROUPS * ATTN_SLOTS
ATTN_WIDTH = N_ATTN_HEADS * HEAD_DIM
ATTN_OUT = ATTN_SLOTS * HEAD_DIM
ATTN_BAND = 128
ROPE_THETA = 500000.0
ROPE_DIM = HEAD_DIM // 4
ROPE_HALF = ROPE_DIM // 2
CONV_WIDTH = D_MODEL // 2
CONV_KERNEL = 31
GMLP_WIDTH = D_MODEL // 2
GMLP_CHUNK = 128
GMLP_GROUPS = 8
GMLP_GROUP_CH = GMLP_WIDTH // GMLP_GROUPS
FFN_HIDDEN = -(-8 * D_MODEL // (3 * 256)) * 256
DEEPNORM_ALPHA = (2 * DEPTH) ** 0.25
LN_EPS = 1e-5
NEG_INF = -1e30
IN_WIDTH = 3 * ATTN_WIDTH + 2 * CONV_WIDTH + 2 * GMLP_WIDTH + 3 * D_MODEL

LANES = 128
SUBLANES = 8
V7X_VMEM_LIMIT_BYTES = 56 * 1024 * 1024

W_TILE = 512
PROJ_TM = 1024
T_Q, T_K, T_V = 0, ATTN_WIDTH // W_TILE, 2 * ATTN_WIDTH // W_TILE
T_CONV = 3 * ATTN_WIDTH // W_TILE
T_GMLP = T_CONV + 2 * CONV_WIDTH // W_TILE
T_GATES = T_GMLP + 2 * GMLP_WIDTH // W_TILE
ATTN_TQ = 128
CONV_TS = 256
CONV_HALO = 32
CONV_ROWS = 128
SGU_TM = 512
MERGE_TM = 256
DOWN_TM = 256
assert W_TILE == ATTN_SLOTS * HEAD_DIM and SEQ % PROJ_TM == 0
assert all(window // dil == ATTN_BAND and PROJ_TM % dil == 0 for window, dil in DILATED_PATTERNS)

_F32 = jnp.float32
_BF16 = jnp.bfloat16


def _params(semantics):
    return pltpu.CompilerParams(dimension_semantics=semantics,
                                vmem_limit_bytes=V7X_VMEM_LIMIT_BYTES)


def _layer_norm(x, g, b):
    mu = jnp.mean(x, axis=-1, keepdims=True)
    xc = x - mu
    var = jnp.mean(xc * xc, axis=-1, keepdims=True)
    return xc * lax.rsqrt(var + LN_EPS) * g + b


def _gelu(x):
    return 0.5 * x * (1.0 + lax.erf(x * math.sqrt(0.5)))


def _rope_table_kernel(cos_ref, sin_lo_ref, sin_hi_ref):
    pos = lax.broadcasted_iota(jnp.int32, (SEQ, LANES), 0).astype(_F32)
    lane = lax.broadcasted_iota(jnp.int32, (SEQ, LANES), 1)
    fidx = (lane & (ROPE_HALF - 1)).astype(_F32)
    inv_freq = jnp.exp(fidx * (-math.log(ROPE_THETA) / ROPE_HALF))
    ang = pos * inv_freq
    c = jnp.cos(ang)
    s = jnp.sin(ang)
    cos_ref[...] = jnp.where(lane < ROPE_DIM, c, 1.0)
    sin_lo_ref[...] = jnp.where(lane < ROPE_HALF, -s, 0.0)
    sin_hi_ref[...] = jnp.where((lane >= ROPE_HALF) & (lane < ROPE_DIM), s, 0.0)


def _rope_tables():
    shp = jax.ShapeDtypeStruct((SEQ, LANES), _F32)
    return pl.pallas_call(_rope_table_kernel, out_shape=(shp, shp, shp), name="rope_tables")()


def _proj_kernel(*refs, n_w, epilogue, dilation=1):
    x_ref = refs[0]
    w_refs = refs[1:1 + n_w]
    n_scratch = 2 if epilogue == "qkv" and dilation > 1 else 1
    extra = refs[1 + n_w:-1 - n_scratch]
    o_ref, wbf_ref = refs[-1 - n_scratch], refs[-n_scratch]

    @pl.when(pl.program_id(1) == 0)
    def _():
        for k, w_ref in enumerate(w_refs):
            wbf_ref[:, k * W_TILE:(k + 1) * W_TILE] = w_ref[...].astype(_BF16)

    acc = jnp.dot(x_ref[...], wbf_ref[...], preferred_element_type=_F32)
    if epilogue == "plain":
        out = acc
    elif epilogue == "gelu":
        out = _gelu(acc)
    elif epilogue == "sigmoid":
        out = jax.nn.sigmoid(acc)
    elif epilogue == "glu":
        out = acc[:, :W_TILE] * jax.nn.sigmoid(acc[:, W_TILE:])
        for k in range(W_TILE // LANES):
            o_ref[k] = out[:, k * LANES:(k + 1) * LANES].astype(o_ref.dtype)
        return
    elif epilogue == "swiglu":
        a = acc[:, :W_TILE]
        out = a * jax.nn.sigmoid(a) * acc[:, W_TILE:]
    elif epilogue == "qkv":
        cos_ref, sin_lo_ref, sin_hi_ref = extra
        c, s_lo, s_hi = cos_ref[...], sin_lo_ref[...], sin_hi_ref[...]
        rows_per_class = PROJ_TM // dilation
        for h in range(n_w * W_TILE // HEAD_DIM):
            cols = slice(h * HEAD_DIM, (h + 1) * HEAD_DIM)
            r = acc[:, cols]
            if h < 2 * ATTN_SLOTS:
                r = (r * c + pltpu.roll(r, HEAD_DIM - ROPE_HALF, 1) * s_lo
                     + pltpu.roll(r, ROPE_HALF, 1) * s_hi)
            if h < ATTN_SLOTS:
                r = r * HEAD_DIM ** -0.5
            if dilation == 1:
                o_ref[:, cols] = r.astype(o_ref.dtype)
            else:
                stage_ref = refs[-1]
                stage_ref[h] = r
                for cls in range(dilation):
                    o_ref[cls * rows_per_class:(cls + 1) * rows_per_class, cols] = (
                        stage_ref[h, pl.ds(cls, rows_per_class, stride=dilation), :].astype(o_ref.dtype))
        return
    else:
        raise ValueError(epilogue)
    o_ref[...] = out.astype(o_ref.dtype)


def _project(xb, weights, n_steps, epilogue, name, tables=(), dilation=1):
    n_w = len(weights)
    w_mode = dict(pipeline_mode=pl.Buffered(1)) if n_steps == 1 else {}
    scratch = [pltpu.VMEM((D_MODEL, n_w * W_TILE), _BF16)]
    if epilogue == "qkv" and dilation > 1:
        scratch.append(pltpu.VMEM((n_w * W_TILE // HEAD_DIM, PROJ_TM, HEAD_DIM), _F32))
    out_tn = W_TILE if epilogue in ("glu", "swiglu") else n_w * W_TILE
    if epilogue == "glu":
        chunks = out_tn // LANES
        out_spec = pl.BlockSpec((chunks, PROJ_TM, LANES), lambda j, m: (j, m, 0))
        out_shape = jax.ShapeDtypeStruct((n_steps * chunks, TOKENS, LANES), _BF16)
    else:
        out_spec = pl.BlockSpec((PROJ_TM, out_tn), lambda j, m: (m, j))
        out_shape = jax.ShapeDtypeStruct((TOKENS, n_steps * out_tn), _BF16)
    seq_tiles = SEQ // PROJ_TM
    w_specs = [pl.BlockSpec((None, D_MODEL, W_TILE),
                            lambda j, m, layer=layer, first=first, stride=stride:
                            (layer, 0, first + stride * j), **w_mode)
               for _, layer, first, stride in weights]
    tab_specs = [pl.BlockSpec((PROJ_TM, LANES), lambda j, m: (m % seq_tiles, 0)) for _ in tables]
    return pl.pallas_call(
        functools.partial(_proj_kernel, n_w=n_w, epilogue=epilogue, dilation=dilation),
        grid=(n_steps, TOKENS // PROJ_TM),
        in_specs=[pl.BlockSpec((PROJ_TM, D_MODEL), lambda j, m: (m, 0))] + w_specs + tab_specs,
        out_specs=out_spec,
        out_shape=out_shape,
        scratch_shapes=scratch,
        compiler_params=_params(("arbitrary", "arbitrary")),
        name=name,
    )(xb, *[w for w, _, _, _ in weights], *tables)


def _attn_kernel(q0, k0, v0, q1, k1, v1, q2, k2, v2, o_ref, *scratch):
    groups = ((q0, k0, v0), (q1, k1, v1), (q2, k2, v2))
    stats = scratch[:3 * N_GROUPS]
    vcs = scratch[3 * N_GROUPS:]
    qi = lax.broadcasted_iota(jnp.int32, (ATTN_TQ, 2 * ATTN_BAND), 0)
    kj = lax.broadcasted_iota(jnp.int32, (ATTN_TQ, 2 * ATTN_BAND), 1)
    band_mask = (kj >= qi) & (kj <= qi + ATTN_BAND)
    causal_mask = (lax.broadcasted_iota(jnp.int32, (ATTN_TQ, ATTN_BAND), 1)
                   <= lax.broadcasted_iota(jnp.int32, (ATTN_TQ, ATTN_BAND), 0))

    def class_rows(ref, dil, cls, start, count):
        per_tile = PROJ_TM // dil
        pieces, pos = [], start
        while pos < start + count:
            tile, off = divmod(pos, per_tile)
            take = min(per_tile - off, start + count - pos)
            row = tile * PROJ_TM + cls * per_tile + off
            pieces.append(ref[row:row + take, :])
            pos += take
        return pieces[0] if len(pieces) == 1 else jnp.concatenate(pieces, axis=0)

    for g, (window, dil) in enumerate(DILATED_PATTERNS):
        q_ref, k_ref, v_ref = groups[g]
        m_ref, den_ref, pv_ref = stats[3 * g:3 * g + 3]
        vc_ref = vcs[g]
        vc_ref[:, HEAD_DIM:] = jnp.ones((SEQ, HEAD_DIM), _BF16)
        vc_ref[:, :HEAD_DIM] = v_ref[...]
        length = SEQ // dil
        for c in range(dil):
            for i in range(length // ATTN_TQ):
                row0 = i * ATTN_TQ
                k_lo = row0 if i == 0 else row0 - ATTN_BAND
                span = row0 + ATTN_TQ - k_lo
                q = class_rows(q_ref, dil, c, row0, ATTN_TQ)
                k = class_rows(k_ref, dil, c, k_lo, span)
                s = lax.dot_general(q, k, (((1,), (1,)), ((), ())), preferred_element_type=_F32)
                s = jnp.where(causal_mask if i == 0 else band_mask, s, NEG_INF)
                m = jnp.max(s, axis=-1, keepdims=True)
                e = jnp.exp(s - m).astype(_BF16)
                r = jnp.dot(e, class_rows(vc_ref, dil, c, k_lo, span), preferred_element_type=_F32)
                if dil == 1:
                    tok = pl.ds(row0, ATTN_TQ)
                else:
                    tok = pl.ds(c + dil * i * ATTN_TQ, ATTN_TQ, stride=dil)
                m_ref[tok, :] = jnp.broadcast_to(m, (ATTN_TQ, HEAD_DIM))
                den_ref[tok, :] = r[:, HEAD_DIM:]
                pv_ref[tok, :] = r[:, :HEAD_DIM]

    rows_per_step = 2 * ATTN_TQ
    for t in range(SEQ // rows_per_step):
        rows = slice(t * rows_per_step, (t + 1) * rows_per_step)
        ms = [stats[3 * g][rows, :] for g in range(N_GROUPS)]
        m_all = functools.reduce(jnp.maximum, ms)
        num = jnp.zeros((rows_per_step, HEAD_DIM), _F32)
        z = jnp.zeros((rows_per_step, HEAD_DIM), _F32)
        for g in range(N_GROUPS):
            w = jnp.exp(ms[g] - m_all)
            z = z + w * stats[3 * g + 1][rows, :]
            num = num + w * stats[3 * g + 2][rows, :]
        o_ref[rows, :] = (num / z).astype(o_ref.dtype)


def _attention(qkv_groups):
    in_specs, operands = [], []
    for qkv in qkv_groups:
        for part in range(3):
            in_specs.append(pl.BlockSpec(
                (SEQ, HEAD_DIM), lambda b, s, part=part: (b, part * ATTN_SLOTS + s)))
            operands.append(qkv)
    tok_f32 = pltpu.VMEM((SEQ, HEAD_DIM), _F32)
    return pl.pallas_call(
        _attn_kernel,
        grid=(BATCH, ATTN_SLOTS),
        in_specs=in_specs,
        out_specs=pl.BlockSpec((SEQ, HEAD_DIM), lambda b, s: (b, s)),
        out_shape=jax.ShapeDtypeStruct((TOKENS, ATTN_OUT), _BF16),
        scratch_shapes=([tok_f32] * (3 * N_GROUPS)
                        + [pltpu.VMEM((SEQ, 2 * HEAD_DIM), _BF16)] * N_GROUPS),
        compiler_params=_params(("arbitrary", "arbitrary")),
        name="dilated_attention",
    )(*operands)


def _conv_block(c, base, cw_ref, cb_ref, cv_ref, hp_ref):
    off = CONV_HALO - (CONV_KERNEL - 1)
    vregs_per_block = CONV_ROWS // SUBLANES
    accs = [jnp.broadcast_to(cb_ref[c], (SUBLANES, LANES))] * vregs_per_block
    for j in range(CONV_KERNEL):
        wj = jnp.broadcast_to(cw_ref[c, j:j + 1, :], (SUBLANES, LANES))
        for v in range(vregs_per_block):
            accs[v] = accs[v] + wj * hp_ref[c, pl.ds(base + off + j + v * SUBLANES, SUBLANES), :]
    for v in range(vregs_per_block):
        cv_ref[c, pl.ds(base + v * SUBLANES, SUBLANES), :] = accs[v]


def _conv_kernel(h_ref, ph_ref, cw_ref, cb_ref, cv_ref, hp_ref):
    first = pl.program_id(1) == 0
    blocks_per_chunk = CONV_TS // CONV_ROWS

    def fill(c, carry):
        hp_ref[c, 0:CONV_HALO, :] = jnp.where(first, 0.0, ph_ref[c].astype(_F32))
        hp_ref[c, CONV_HALO:, :] = h_ref[c].astype(_F32)
        return carry

    lax.fori_loop(0, CONV_WIDTH // LANES, fill, 0)

    def block(i, carry):
        _conv_block(i // blocks_per_chunk, (i % blocks_per_chunk) * CONV_ROWS,
                    cw_ref, cb_ref, cv_ref, hp_ref)
        return carry

    lax.fori_loop(0, (CONV_WIDTH // LANES) * blocks_per_chunk, block, 0)


def _conformer_conv(h_cm, w_dw, b_dw, layer):
    n_chunks = CONV_WIDTH // LANES
    tiles = SEQ // CONV_TS
    halo_per_tile = CONV_TS // CONV_HALO
    w_chunks = jnp.swapaxes(w_dw.reshape(DEPTH, CONV_KERNEL, n_chunks, LANES), 1, 2)
    b_chunks = b_dw.reshape(DEPTH, n_chunks, 1, LANES)
    return pl.pallas_call(
        _conv_kernel,
        grid=(BATCH, tiles),
        in_specs=[pl.BlockSpec((n_chunks, CONV_TS, LANES), lambda b, i: (0, b * tiles + i, 0)),
                  pl.BlockSpec((n_chunks, CONV_HALO, LANES),
                               lambda b, i: (0, jnp.maximum((b * tiles + i) * halo_per_tile - 1, 0), 0)),
                  pl.BlockSpec((None, n_chunks, CONV_KERNEL, LANES), lambda b, i: (layer, 0, 0, 0)),
                  pl.BlockSpec((None, n_chunks, 1, LANES), lambda b, i: (layer, 0, 0, 0))],
        out_specs=pl.BlockSpec((n_chunks, CONV_TS, LANES), lambda b, i: (0, b * tiles + i, 0)),
        out_shape=jax.ShapeDtypeStruct((n_chunks, TOKENS, LANES), _F32),
        scratch_shapes=[pltpu.VMEM((n_chunks, CONV_HALO + CONV_TS, LANES), _F32)],
        compiler_params=_params(("arbitrary", "arbitrary")),
        name="conformer_conv",
    )(h_cm, h_cm, w_chunks, b_chunks)


def _sgu_kernel(z_ref, ws_ref, bs_ref, g_ref, b_ref, o_ref):
    chunks = SGU_TM // GMLP_CHUNK
    vn = _layer_norm(z_ref[:, GMLP_WIDTH:].astype(_F32), g_ref[...], b_ref[...]).astype(_BF16)
    row = lax.broadcasted_iota(jnp.int32, (GMLP_CHUNK, GMLP_CHUNK), 0)
    col = lax.broadcasted_iota(jnp.int32, (GMLP_CHUNK, GMLP_CHUNK), 1)
    causal = col <= row
    for g in range(GMLP_GROUPS):
        cols = slice(g * GMLP_GROUP_CH, (g + 1) * GMLP_GROUP_CH)
        wm = jnp.where(causal, ws_ref[g], 0.0).astype(_BF16)
        rhs = jnp.concatenate(
            [vn[n * GMLP_CHUNK:(n + 1) * GMLP_CHUNK, cols] for n in range(chunks)], axis=1)
        s = jnp.dot(wm, rhs, preferred_element_type=_F32) + bs_ref[:, g:g + 1]
        for n in range(chunks):
            rows = slice(n * GMLP_CHUNK, (n + 1) * GMLP_CHUNK)
            u = z_ref[rows, cols].astype(_F32)
            o_ref[rows, cols] = (u * s[:, n * GMLP_GROUP_CH:(n + 1) * GMLP_GROUP_CH]).astype(o_ref.dtype)


def _spatial_gating(z, w_spatial, b_spatial_t, ln_g, ln_b, layer):
    vec = pl.BlockSpec((None, 1, GMLP_WIDTH), lambda i: (layer, 0, 0))
    return pl.pallas_call(
        _sgu_kernel,
        grid=(TOKENS // SGU_TM,),
        in_specs=[pl.BlockSpec((SGU_TM, 2 * GMLP_WIDTH), lambda i: (i, 0)),
                  pl.BlockSpec((None, GMLP_GROUPS, GMLP_CHUNK, GMLP_CHUNK), lambda i: (layer, 0, 0, 0)),
                  pl.BlockSpec((None, GMLP_CHUNK, GMLP_GROUPS), lambda i: (layer, 0, 0)),
                  vec, vec],
        out_specs=pl.BlockSpec((SGU_TM, GMLP_WIDTH), lambda i: (i, 0)),
        out_shape=jax.ShapeDtypeStruct((TOKENS, GMLP_WIDTH), _BF16),
        compiler_params=_params(("arbitrary",)),
        name="spatial_gating",
    )(z, w_spatial, b_spatial_t, ln_g.reshape(DEPTH, 1, GMLP_WIDTH),
      ln_b.reshape(DEPTH, 1, GMLP_WIDTH))


def _merge_kernel(attn_ref, cv_ref, hg_ref, ga_ref, gc_ref, gm_ref, x_ref,
                  wa_ref, wc_ref, wg_ref, wo_ref, cg_ref, cb_ref, g_ref, b_ref, xo_ref, xob_ref):
    conv = jnp.concatenate([cv_ref[c] for c in range(CONV_WIDTH // LANES)], axis=1)
    hc = _layer_norm(conv, cg_ref[...], cb_ref[...])
    hc = (hc * jax.nn.sigmoid(hc)).astype(_BF16)
    ya = jnp.dot(attn_ref[...], wa_ref[...], preferred_element_type=_F32)
    merged = ga_ref[...].astype(_F32) * ya
    yc = jnp.dot(hc, wc_ref[...], preferred_element_type=_F32)
    merged = merged + gc_ref[...].astype(_F32) * yc
    yg = jnp.dot(hg_ref[...], wg_ref[...], preferred_element_type=_F32)
    merged = merged + gm_ref[...].astype(_F32) * yg
    y = jnp.dot(merged.astype(_BF16), wo_ref[...], preferred_element_type=_F32)
    out = _layer_norm(DEEPNORM_ALPHA * x_ref[...] + y, g_ref[...], b_ref[...])
    xo_ref[...] = out
    xob_ref[...] = out.astype(_BF16)


def _resident(shape):
    return pl.BlockSpec(shape, lambda i: (0,) * len(shape), pipeline_mode=pl.Buffered(1))


def _merge(attn, conv_cm, hgmlp, gates, x, wa, wc, wg, wo, conv_ln_g, conv_ln_b, ln_g, ln_b, layer):
    tm = MERGE_TM
    row = lambda width: pl.BlockSpec((tm, width), lambda i: (i, 0))
    gate = lambda k: pl.BlockSpec((tm, D_MODEL), lambda i: (i, k))
    vec = lambda width: pl.BlockSpec((None, 1, width), lambda i: (layer, 0, 0))
    out_spec = pl.BlockSpec((tm, D_MODEL), lambda i: (i, 0))
    return pl.pallas_call(
        _merge_kernel,
        grid=(TOKENS // tm,),
        in_specs=[row(ATTN_OUT),
                  pl.BlockSpec((CONV_WIDTH // LANES, tm, LANES), lambda i: (0, i, 0)),
                  row(GMLP_WIDTH), gate(0), gate(1), gate(2), row(D_MODEL),
                  _resident((ATTN_OUT, D_MODEL)), _resident((CONV_WIDTH, D_MODEL)),
                  _resident((GMLP_WIDTH, D_MODEL)), _resident((D_MODEL, D_MODEL)),
                  vec(CONV_WIDTH), vec(CONV_WIDTH), vec(D_MODEL), vec(D_MODEL)],
        out_specs=(out_spec, out_spec),
        out_shape=(jax.ShapeDtypeStruct((TOKENS, D_MODEL), _F32),
                   jax.ShapeDtypeStruct((TOKENS, D_MODEL), _BF16)),
        compiler_params=_params(("arbitrary",)),
        name="merge_out_ln1",
    )(attn, conv_cm, hgmlp, gates, gates, gates, x, wa, wc, wg, wo,
      conv_ln_g.reshape(DEPTH, 1, CONV_WIDTH), conv_ln_b.reshape(DEPTH, 1, CONV_WIDTH),
      ln_g.reshape(DEPTH, 1, D_MODEL), ln_b.reshape(DEPTH, 1, D_MODEL))


def _ffn_down_kernel(h_ref, x_ref, w_ref, g_ref, b_ref, xo_ref, xob_ref):
    f = jnp.dot(h_ref[...], w_ref[...], preferred_element_type=_F32)
    out = _layer_norm(DEEPNORM_ALPHA * x_ref[...] + f, g_ref[...], b_ref[...])
    xo_ref[...] = out
    xob_ref[...] = out.astype(_BF16)


def _ffn_down(h, x, w_down, ln_g, ln_b, layer):
    tm = DOWN_TM
    vec = pl.BlockSpec((None, 1, D_MODEL), lambda i: (layer, 0, 0))
    out_spec = pl.BlockSpec((tm, D_MODEL), lambda i: (i, 0))
    return pl.pallas_call(
        _ffn_down_kernel,
        grid=(TOKENS // tm,),
        in_specs=[pl.BlockSpec((tm, FFN_HIDDEN), lambda i: (i, 0)),
                  pl.BlockSpec((tm, D_MODEL), lambda i: (i, 0)),
                  _resident((FFN_HIDDEN, D_MODEL)), vec, vec],
        out_specs=(out_spec, out_spec),
        out_shape=(jax.ShapeDtypeStruct((TOKENS, D_MODEL), _F32),
                   jax.ShapeDtypeStruct((TOKENS, D_MODEL), _BF16)),
        compiler_params=_params(("arbitrary",)),
        name="ffn_down_ln2",
    )(h, x, w_down, ln_g.reshape(DEPTH, 1, D_MODEL), ln_b.reshape(DEPTH, 1, D_MODEL))


def kernel(x, w_in, w_attn_proj, conv_dw, conv_dw_b, conv_ln_g, conv_ln_b, w_conv_proj, gmlp_ln_g, gmlp_ln_b, w_spatial, b_spatial, w_gmlp_proj, w_out, ln1_g, ln1_b, w_ffn_gate, w_ffn_up, w_ffn_down, ln2_g, ln2_b):
    assert x.shape == (BATCH, SEQ, D_MODEL) and w_in.shape == (DEPTH, D_MODEL, IN_WIDTH)
    tables = _rope_tables()
    xf = x.reshape(TOKENS, D_MODEL)
    xb = xf.astype(_BF16)
    b_spatial_t = jnp.swapaxes(b_spatial, 1, 2)
    for layer in range(DEPTH):
        qkv_groups = [
            _project(xb, [(w_in, layer, T_Q + g, 0), (w_in, layer, T_K + g, 0), (w_in, layer, T_V + g, 0)],
                     1, "qkv", f"proj_qkv_g{g}", tables, dilation=dil)
            for g, (_, dil) in enumerate(DILATED_PATTERNS)]
        h = _project(xb, [(w_in, layer, T_CONV, 1), (w_in, layer, T_CONV + CONV_WIDTH // W_TILE, 1)],
                     CONV_WIDTH // W_TILE, "glu", "proj_glu")
        z = _project(xb, [(w_in, layer, T_GMLP, 2), (w_in, layer, T_GMLP + 1, 2)],
                     2 * GMLP_WIDTH // (2 * W_TILE), "gelu", "proj_gelu")
        gates = _project(xb, [(w_in, layer, T_GATES, 2), (w_in, layer, T_GATES + 1, 2)],
                         3 * D_MODEL // (2 * W_TILE), "sigmoid", "proj_gates")
        conv_cm = _conformer_conv(h, conv_dw, conv_dw_b, layer)
        attn = _attention(qkv_groups)
        hgmlp = _spatial_gating(z, w_spatial, b_spatial_t, gmlp_ln_g, gmlp_ln_b, layer)
        xf, xb = _merge(attn, conv_cm, hgmlp, gates, xf,
                        w_attn_proj[layer].astype(_BF16), w_conv_proj[layer].astype(_BF16),
                        w_gmlp_proj[layer].astype(_BF16), w_out[layer].astype(_BF16),
                        conv_ln_g, conv_ln_b, ln1_g, ln1_b, layer)
        hid = _project(xb, [(w_ffn_gate, layer, 0, 1), (w_ffn_up, layer, 0, 1)],
                       FFN_HIDDEN // W_TILE, "swiglu", "ffn_up")
        xf, xb = _ffn_down(hid, xf, w_ffn_down[layer].astype(_BF16), ln2_g, ln2_b, layer)
    return xf.reshape(BATCH, SEQ, D_MODEL)
```

```python
import functools
import math

import jax
import jax.numpy as jnp
from jax import lax
from jax.experimental import pallas as pl
from jax.experimental.pallas import tpu as pltpu

D_MODEL = 2048
BATCH = 4
SEQ = 2048
DEPTH = 2
TOKENS = BATCH * SEQ
HEAD_DIM = 128
ATTN_SLOTS = 4
DILATED_PATTERNS = ((128, 1), (512, 4), (2048, 16))
N_GROUPS = len(DILATED_PATTERNS)
N_ATTN_HEADS = N_GROUPS * ATTN_SLOTS
ATTN_WIDTH = N_ATTN_HEADS * HEAD_DIM
ATTN_OUT = ATTN_SLOTS * HEAD_DIM
ATTN_BAND = 128
ROPE_THETA = 500000.0
ROPE_DIM = HEAD_DIM // 4
ROPE_HALF = ROPE_DIM // 2
CONV_WIDTH = D_MODEL // 2
CONV_KERNEL = 31
GMLP_WIDTH = D_MODEL // 2
GMLP_CHUNK = 128
GMLP_GROUPS = 8
GMLP_GROUP_CH = GMLP_WIDTH // GMLP_GROUPS
FFN_HIDDEN = -(-8 * D_MODEL // (3 * 256)) * 256
DEEPNORM_ALPHA = (2 * DEPTH) ** 0.25
LN_EPS = 1e-5
NEG_INF = -1e30
IN_WIDTH = 3 * ATTN_WIDTH + 2 * CONV_WIDTH + 2 * GMLP_WIDTH + 3 * D_MODEL

LANES = 128
SUBLANES = 8
V7X_VMEM_LIMIT_BYTES = 56 * 1024 * 1024

W_TILE = 512
PROJ_TM = 1024
T_Q, T_K, T_V = 0, ATTN_WIDTH // W_TILE, 2 * ATTN_WIDTH // W_TILE
T_CONV = 3 * ATTN_WIDTH // W_TILE
T_GMLP = T_CONV + 2 * CONV_WIDTH // W_TILE
T_GATES = T_GMLP + 2 * GMLP_WIDTH // W_TILE
ATTN_TQ = 128
CONV_TS = 256
CONV_HALO = 32
CONV_ROWS = 128
SGU_TM = 512
CAST_ROWS = 512
MERGE_TM = 256
DOWN_TM = 256
assert W_TILE == ATTN_SLOTS * HEAD_DIM and SEQ % PROJ_TM == 0
assert all(window // dil == ATTN_BAND and PROJ_TM % dil == 0 for window, dil in DILATED_PATTERNS)

_F32 = jnp.float32
_BF16 = jnp.bfloat16


def _params(semantics):
    return pltpu.CompilerParams(dimension_semantics=semantics,
                                vmem_limit_bytes=V7X_VMEM_LIMIT_BYTES)


def _layer_norm(x, g, b):
    mu = jnp.mean(x, axis=-1, keepdims=True)
    xc = x - mu
    var = jnp.mean(xc * xc, axis=-1, keepdims=True)
    return xc * lax.rsqrt(var + LN_EPS) * g + b


def _gelu(x):
    return 0.5 * x * (1.0 + lax.erf(x * math.sqrt(0.5)))


def _rope_table_kernel(cos_ref, sin_lo_ref, sin_hi_ref):
    pos = lax.broadcasted_iota(jnp.int32, (SEQ, LANES), 0).astype(_F32)
    lane = lax.broadcasted_iota(jnp.int32, (SEQ, LANES), 1)
    fidx = (lane & (ROPE_HALF - 1)).astype(_F32)
    inv_freq = jnp.exp(fidx * (-math.log(ROPE_THETA) / ROPE_HALF))
    ang = pos * inv_freq
    c = jnp.cos(ang)
    s = jnp.sin(ang)
    cos_ref[...] = jnp.where(lane < ROPE_DIM, c, 1.0)
    sin_lo_ref[...] = jnp.where(lane < ROPE_HALF, -s, 0.0)
    sin_hi_ref[...] = jnp.where((lane >= ROPE_HALF) & (lane < ROPE_DIM), s, 0.0)


def _rope_tables():
    shp = jax.ShapeDtypeStruct((SEQ, LANES), _F32)
    return pl.pallas_call(_rope_table_kernel, out_shape=(shp, shp, shp), name="rope_tables")()


def _proj_kernel(*refs, n_w, epilogue, dilation=1):
    x_ref = refs[0]
    w_refs = refs[1:1 + n_w]
    n_scratch = 2 if epilogue == "qkv" and dilation > 1 else 1
    extra = refs[1 + n_w:-1 - n_scratch]
    o_ref, wbf_ref = refs[-1 - n_scratch], refs[-n_scratch]

    @pl.when(pl.program_id(1) == 0)
    def _():
        for k, w_ref in enumerate(w_refs):
            wbf_ref[:, k * W_TILE:(k + 1) * W_TILE] = w_ref[...].astype(_BF16)

    acc = jnp.dot(x_ref[...], wbf_ref[...], preferred_element_type=_F32)
    if epilogue == "gelu_sgu":
        ws_ref, bs_ref, g_ref, b_ref = extra
        z = _gelu(acc)
        vn = _layer_norm(z[:, GMLP_WIDTH:], g_ref[...], b_ref[...]).astype(_BF16)
        chunks = x_ref.shape[0] // GMLP_CHUNK
        row = lax.broadcasted_iota(jnp.int32, (GMLP_CHUNK, GMLP_CHUNK), 0)
        col = lax.broadcasted_iota(jnp.int32, (GMLP_CHUNK, GMLP_CHUNK), 1)
        for g in range(GMLP_GROUPS):
            cols = slice(g * GMLP_GROUP_CH, (g + 1) * GMLP_GROUP_CH)
            wm = jnp.where(col <= row, ws_ref[g], 0.0).astype(_BF16)
            rhs = jnp.concatenate(
                [vn[n * GMLP_CHUNK:(n + 1) * GMLP_CHUNK, cols] for n in range(chunks)], axis=1)
            s = jnp.dot(wm, rhs, preferred_element_type=_F32) + bs_ref[:, g:g + 1]
            for n in range(chunks):
                rows = slice(n * GMLP_CHUNK, (n + 1) * GMLP_CHUNK)
                o_ref[rows, cols] = (z[rows, cols] * s[:, n * GMLP_GROUP_CH:(n + 1) * GMLP_GROUP_CH]
                                     ).astype(o_ref.dtype)
        return
    elif epilogue == "sigmoid":
        out = jax.nn.sigmoid(acc)
    elif epilogue == "glu":
        out = acc[:, :W_TILE] * jax.nn.sigmoid(acc[:, W_TILE:])
        for k in range(W_TILE // LANES):
            o_ref[k] = out[:, k * LANES:(k + 1) * LANES].astype(o_ref.dtype)
        return
    elif epilogue == "swiglu":
        a = acc[:, :W_TILE]
        out = a * jax.nn.sigmoid(a) * acc[:, W_TILE:]
    elif epilogue == "qkv":
        cos_ref, sin_lo_ref, sin_hi_ref = extra
        c, s_lo, s_hi = cos_ref[...], sin_lo_ref[...], sin_hi_ref[...]
        rows_per_class = PROJ_TM // dilation
        for h in range(n_w * W_TILE // HEAD_DIM):
            cols = slice(h * HEAD_DIM, (h + 1) * HEAD_DIM)
            r = acc[:, cols]
            if h < 2 * ATTN_SLOTS:
                r = (r * c + pltpu.roll(r, HEAD_DIM - ROPE_HALF, 1) * s_lo
                     + pltpu.roll(r, ROPE_HALF, 1) * s_hi)
            if h < ATTN_SLOTS:
                r = r * HEAD_DIM ** -0.5
            if dilation == 1:
                o_ref[:, cols] = r.astype(o_ref.dtype)
            else:
                stage_ref = refs[-1]
                stage_ref[h] = r
                for cls in range(dilation):
                    o_ref[cls * rows_per_class:(cls + 1) * rows_per_class, cols] = (
                        stage_ref[h, pl.ds(cls, rows_per_class, stride=dilation), :].astype(o_ref.dtype))
        return
    else:
        raise ValueError(epilogue)
    o_ref[...] = out.astype(o_ref.dtype)


def _project(xb, weights, n_steps, epilogue, name, extra=(), dilation=1, tm=PROJ_TM):
    n_w = len(weights)
    w_mode = dict(pipeline_mode=pl.Buffered(1)) if n_steps == 1 else {}
    scratch = [pltpu.VMEM((D_MODEL, n_w * W_TILE), _BF16)]
    if epilogue == "qkv" and dilation > 1:
        assert tm == PROJ_TM
        scratch.append(pltpu.VMEM((n_w * W_TILE // HEAD_DIM, PROJ_TM, HEAD_DIM), _F32))
    out_tn = {"glu": W_TILE, "swiglu": W_TILE, "gelu_sgu": GMLP_WIDTH}.get(epilogue, n_w * W_TILE)
    if epilogue == "glu":
        chunks = out_tn // LANES
        out_spec = pl.BlockSpec((chunks, tm, LANES), lambda j, m: (j, m, 0))
        out_shape = jax.ShapeDtypeStruct((n_steps * chunks, TOKENS, LANES), _BF16)
    else:
        out_spec = pl.BlockSpec((tm, out_tn), lambda j, m: (m, j))
        out_shape = jax.ShapeDtypeStruct((TOKENS, n_steps * out_tn), _BF16)
    w_specs = [pl.BlockSpec((None, D_MODEL, W_TILE),
                            lambda j, m, layer=layer, first=first, stride=stride:
                            (layer, 0, first + stride * j), **w_mode)
               for _, layer, first, stride in weights]
    return pl.pallas_call(
        functools.partial(_proj_kernel, n_w=n_w, epilogue=epilogue, dilation=dilation),
        grid=(n_steps, TOKENS // tm),
        in_specs=[pl.BlockSpec((tm, D_MODEL), lambda j, m: (m, 0))] + w_specs + [s for _, s in extra],
        out_specs=out_spec,
        out_shape=out_shape,
        scratch_shapes=scratch,
        compiler_params=_params(("arbitrary", "arbitrary")),
        name=name,
    )(xb, *[w for w, _, _, _ in weights], *[a for a, _ in extra])


def _attn_kernel(q0, k0, v0, q1, k1, v1, q2, k2, v2, o_ref, *scratch):
    groups = ((q0, k0, v0), (q1, k1, v1), (q2, k2, v2))
    stats = scratch[:3 * N_GROUPS]
    vcs = scratch[3 * N_GROUPS:]
    qi = lax.broadcasted_iota(jnp.int32, (ATTN_TQ, 2 * ATTN_BAND), 0)
    kj = lax.broadcasted_iota(jnp.int32, (ATTN_TQ, 2 * ATTN_BAND), 1)
    band_mask = (kj >= qi) & (kj <= qi + ATTN_BAND)
    causal_mask = (lax.broadcasted_iota(jnp.int32, (ATTN_TQ, ATTN_BAND), 1)
                   <= lax.broadcasted_iota(jnp.int32, (ATTN_TQ, ATTN_BAND), 0))

    def class_rows(ref, dil, cls, start, count):
        per_tile = PROJ_TM // dil
        pieces, pos = [], start
        while pos < start + count:
            tile, off = divmod(pos, per_tile)
            take = min(per_tile - off, start + count - pos)
            row = tile * PROJ_TM + cls * per_tile + off
            pieces.append(ref[row:row + take, :])
            pos += take
        return pieces[0] if len(pieces) == 1 else jnp.concatenate(pieces, axis=0)

    for g, (window, dil) in enumerate(DILATED_PATTERNS):
        q_ref, k_ref, v_ref = groups[g]
        m_ref, den_ref, pv_ref = stats[3 * g:3 * g + 3]
        vc_ref = vcs[g]
        vc_ref[:, HEAD_DIM:] = jnp.ones((SEQ, HEAD_DIM), _BF16)
        vc_ref[:, :HEAD_DIM] = v_ref[...]
        length = SEQ // dil
        for c in range(dil):
            for i in range(length // ATTN_TQ):
                row0 = i * ATTN_TQ
                k_lo = row0 if i == 0 else row0 - ATTN_BAND
                span = row0 + ATTN_TQ - k_lo
                q = class_rows(q_ref, dil, c, row0, ATTN_TQ)
                k = class_rows(k_ref, dil, c, k_lo, span)
                s = lax.dot_general(q, k, (((1,), (1,)), ((), ())), preferred_element_type=_F32)
                s = jnp.where(causal_mask if i == 0 else band_mask, s, NEG_INF)
                m = jnp.max(s, axis=-1, keepdims=True)
                e = jnp.exp(s - m).astype(_BF16)
                r = jnp.dot(e, class_rows(vc_ref, dil, c, k_lo, span), preferred_element_type=_F32)
                if dil == 1:
                    tok = pl.ds(row0, ATTN_TQ)
                else:
                    tok = pl.ds(c + dil * i * ATTN_TQ, ATTN_TQ, stride=dil)
                m_ref[tok, :] = jnp.broadcast_to(m, (ATTN_TQ, HEAD_DIM))
                den_ref[tok, :] = r[:, HEAD_DIM:]
                pv_ref[tok, :] = r[:, :HEAD_DIM]

    rows_per_step = 2 * ATTN_TQ
    for t in range(SEQ // rows_per_step):
        rows = slice(t * rows_per_step, (t + 1) * rows_per_step)
        ms = [stats[3 * g][rows, :] for g in range(N_GROUPS)]
        m_all = functools.reduce(jnp.maximum, ms)
        num = jnp.zeros((rows_per_step, HEAD_DIM), _F32)
        z = jnp.zeros((rows_per_step, HEAD_DIM), _F32)
        for g in range(N_GROUPS):
            w = jnp.exp(ms[g] - m_all)
            z = z + w * stats[3 * g + 1][rows, :]
            num = num + w * stats[3 * g + 2][rows, :]
        o_ref[rows, :] = (num / z).astype(o_ref.dtype)


def _attention(qkv_groups):
    in_specs, operands = [], []
    for qkv in qkv_groups:
        for part in range(3):
            in_specs.append(pl.BlockSpec(
                (SEQ, HEAD_DIM), lambda b, s, part=part: (b, part * ATTN_SLOTS + s)))
            operands.append(qkv)
    tok_f32 = pltpu.VMEM((SEQ, HEAD_DIM), _F32)
    return pl.pallas_call(
        _attn_kernel,
        grid=(BATCH, ATTN_SLOTS),
        in_specs=in_specs,
        out_specs=pl.BlockSpec((SEQ, HEAD_DIM), lambda b, s: (b, s)),
        out_shape=jax.ShapeDtypeStruct((TOKENS, ATTN_OUT), _BF16),
        scratch_shapes=([tok_f32] * (3 * N_GROUPS)
                        + [pltpu.VMEM((SEQ, 2 * HEAD_DIM), _BF16)] * N_GROUPS),
        compiler_params=_params(("arbitrary", "arbitrary")),
        name="dilated_attention",
    )(*operands)


def _conv_block(c, base, cw_ref, cb_ref, cv_ref, hp_ref):
    off = CONV_HALO - (CONV_KERNEL - 1)
    vregs_per_block = CONV_ROWS // SUBLANES
    accs = [jnp.broadcast_to(cb_ref[c], (SUBLANES, LANES))] * vregs_per_block
    for j in range(CONV_KERNEL):
        wj = jnp.broadcast_to(cw_ref[c, j:j + 1, :], (SUBLANES, LANES))
        for v in range(vregs_per_block):
            accs[v] = accs[v] + wj * hp_ref[c, pl.ds(base + off + j + v * SUBLANES, SUBLANES), :]
    for v in range(vregs_per_block):
        cv_ref[c, pl.ds(base + v * SUBLANES, SUBLANES), :] = accs[v]


def _conv_kernel(h_ref, ph_ref, cw_ref, cb_ref, cv_ref, hp_ref):
    first = pl.program_id(1) == 0
    blocks_per_chunk = CONV_TS // CONV_ROWS

    def fill(c, carry):
        hp_ref[c, 0:CONV_HALO, :] = jnp.where(first, 0.0, ph_ref[c].astype(_F32))
        hp_ref[c, CONV_HALO:, :] = h_ref[c].astype(_F32)
        return carry

    lax.fori_loop(0, CONV_WIDTH // LANES, fill, 0)

    def block(i, carry):
        _conv_block(i // blocks_per_chunk, (i % blocks_per_chunk) * CONV_ROWS,
                    cw_ref, cb_ref, cv_ref, hp_ref)
        return carry

    lax.fori_loop(0, (CONV_WIDTH // LANES) * blocks_per_chunk, block, 0)


def _conformer_conv(h_cm, w_dw, b_dw, layer):
    n_chunks = CONV_WIDTH // LANES
    tiles = SEQ // CONV_TS
    halo_per_tile = CONV_TS // CONV_HALO
    w_chunks = jnp.swapaxes(w_dw.reshape(DEPTH, CONV_KERNEL, n_chunks, LANES), 1, 2)
    b_chunks = b_dw.reshape(DEPTH, n_chunks, 1, LANES)
    return pl.pallas_call(
        _conv_kernel,
        grid=(BATCH, tiles),
        in_specs=[pl.BlockSpec((n_chunks, CONV_TS, LANES), lambda b, i: (0, b * tiles + i, 0)),
                  pl.BlockSpec((n_chunks, CONV_HALO, LANES),
                               lambda b, i: (0, jnp.maximum((b * tiles + i) * halo_per_tile - 1, 0), 0)),
                  pl.BlockSpec((None, n_chunks, CONV_KERNEL, LANES), lambda b, i: (layer, 0, 0, 0)),
                  pl.BlockSpec((None, n_chunks, 1, LANES), lambda b, i: (layer, 0, 0, 0))],
        out_specs=pl.BlockSpec((n_chunks, CONV_TS, LANES), lambda b, i: (0, b * tiles + i, 0)),
        out_shape=jax.ShapeDtypeStruct((n_chunks, TOKENS, LANES), _F32),
        scratch_shapes=[pltpu.VMEM((n_chunks, CONV_HALO + CONV_TS, LANES), _F32)],
        compiler_params=_params(("arbitrary", "arbitrary")),
        name="conformer_conv",
    )(h_cm, h_cm, w_chunks, b_chunks)


def _cast_kernel(w_ref, o_ref):
    o_ref[...] = w_ref[...].astype(o_ref.dtype)


def _to_bf16(w):
    depth, k, n = w.shape
    spec = pl.BlockSpec((None, CAST_ROWS, n), lambda l, i: (l, i, 0))
    return pl.pallas_call(
        _cast_kernel,
        grid=(depth, k // CAST_ROWS),
        in_specs=[spec],
        out_specs=spec,
        out_shape=jax.ShapeDtypeStruct(w.shape, _BF16),
        compiler_params=_params(("arbitrary", "arbitrary")),
        name="cast_weights",
    )(w)


def _merge_kernel(attn_ref, cv_ref, hg_ref, ga_ref, gc_ref, gm_ref, x_ref,
                  wa_ref, wc_ref, wg_ref, wo_ref, cg_ref, cb_ref, g_ref, b_ref, xo_ref, xob_ref):
    conv = jnp.concatenate([cv_ref[c] for c in range(CONV_WIDTH // LANES)], axis=1)
    hc = _layer_norm(conv, cg_ref[...], cb_ref[...])
    hc = (hc * jax.nn.sigmoid(hc)).astype(_BF16)
    ya = jnp.dot(attn_ref[...], wa_ref[...], preferred_element_type=_F32)
    merged = ga_ref[...].astype(_F32) * ya
    yc = jnp.dot(hc, wc_ref[...], preferred_element_type=_F32)
    merged = merged + gc_ref[...].astype(_F32) * yc
    yg = jnp.dot(hg_ref[...], wg_ref[...], preferred_element_type=_F32)
    merged = merged + gm_ref[...].astype(_F32) * yg
    y = jnp.dot(merged.astype(_BF16), wo_ref[...], preferred_element_type=_F32)
    out = _layer_norm(DEEPNORM_ALPHA * x_ref[...] + y, g_ref[...], b_ref[...])
    xo_ref[...] = out
    xob_ref[...] = out.astype(_BF16)


def _resident(shape, layer):
    return pl.BlockSpec((None,) + shape, lambda i: (layer, 0, 0), pipeline_mode=pl.Buffered(1))


def _merge(attn, conv_cm, hgmlp, gates, x, wa, wc, wg, wo, conv_ln_g, conv_ln_b, ln_g, ln_b, layer):
    tm = MERGE_TM
    row = lambda width: pl.BlockSpec((tm, width), lambda i: (i, 0))
    gate = lambda k: pl.BlockSpec((tm, D_MODEL), lambda i: (i, k))
    vec = lambda width: pl.BlockSpec((None, 1, width), lambda i: (layer, 0, 0))
    out_spec = pl.BlockSpec((tm, D_MODEL), lambda i: (i, 0))
    return pl.pallas_call(
        _merge_kernel,
        grid=(TOKENS // tm,),
        in_specs=[row(ATTN_OUT),
                  pl.BlockSpec((CONV_WIDTH // LANES, tm, LANES), lambda i: (0, i, 0)),
                  row(GMLP_WIDTH), gate(0), gate(1), gate(2), row(D_MODEL),
                  _resident((ATTN_OUT, D_MODEL), layer), _resident((CONV_WIDTH, D_MODEL), layer),
                  _resident((GMLP_WIDTH, D_MODEL), layer), _resident((D_MODEL, D_MODEL), layer),
                  vec(CONV_WIDTH), vec(CONV_WIDTH), vec(D_MODEL), vec(D_MODEL)],
        out_specs=(out_spec, out_spec),
        out_shape=(jax.ShapeDtypeStruct((TOKENS, D_MODEL), _F32),
                   jax.ShapeDtypeStruct((TOKENS, D_MODEL), _BF16)),
        compiler_params=_params(("arbitrary",)),
        name="merge_out_ln1",
    )(attn, conv_cm, hgmlp, gates, gates, gates, x, wa, wc, wg, wo,
      conv_ln_g.reshape(DEPTH, 1, CONV_WIDTH), conv_ln_b.reshape(DEPTH, 1, CONV_WIDTH),
      ln_g.reshape(DEPTH, 1, D_MODEL), ln_b.reshape(DEPTH, 1, D_MODEL))


def _ffn_down_kernel(h_ref, x_ref, w_ref, g_ref, b_ref, xo_ref, xob_ref):
    f = jnp.dot(h_ref[...], w_ref[...], preferred_element_type=_F32)
    out = _layer_norm(DEEPNORM_ALPHA * x_ref[...] + f, g_ref[...], b_ref[...])
    xo_ref[...] = out
    xob_ref[...] = out.astype(_BF16)


def _ffn_down(h, x, w_down, ln_g, ln_b, layer):
    tm = DOWN_TM
    vec = pl.BlockSpec((None, 1, D_MODEL), lambda i: (layer, 0, 0))
    out_spec = pl.BlockSpec((tm, D_MODEL), lambda i: (i, 0))
    return pl.pallas_call(
        _ffn_down_kernel,
        grid=(TOKENS // tm,),
        in_specs=[pl.BlockSpec((tm, FFN_HIDDEN), lambda i: (i, 0)),
                  pl.BlockSpec((tm, D_MODEL), lambda i: (i, 0)),
                  _resident((FFN_HIDDEN, D_MODEL), layer), vec, vec],
        out_specs=(out_spec, out_spec),
        out_shape=(jax.ShapeDtypeStruct((TOKENS, D_MODEL), _F32),
                   jax.ShapeDtypeStruct((TOKENS, D_MODEL), _BF16)),
        compiler_params=_params(("arbitrary",)),
        name="ffn_down_ln2",
    )(h, x, w_down, ln_g.reshape(DEPTH, 1, D_MODEL), ln_b.reshape(DEPTH, 1, D_MODEL))


def kernel(x, w_in, w_attn_proj, conv_dw, conv_dw_b, conv_ln_g, conv_ln_b, w_conv_proj, gmlp_ln_g, gmlp_ln_b, w_spatial, b_spatial, w_gmlp_proj, w_out, ln1_g, ln1_b, w_ffn_gate, w_ffn_up, w_ffn_down, ln2_g, ln2_b):
    assert x.shape == (BATCH, SEQ, D_MODEL) and w_in.shape == (DEPTH, D_MODEL, IN_WIDTH)
    seq_tiles = SEQ // PROJ_TM
    table_spec = pl.BlockSpec((PROJ_TM, LANES), lambda j, m: (m % seq_tiles, 0))
    tables = [(t, table_spec) for t in _rope_tables()]
    xf = x.reshape(TOKENS, D_MODEL)
    xb = xf.astype(_BF16)
    b_spatial_t = jnp.swapaxes(b_spatial, 1, 2)
    wa, wc, wg, wo, wd = [_to_bf16(w) for w in (w_attn_proj, w_conv_proj, w_gmlp_proj, w_out, w_ffn_down)]
    for layer in range(DEPTH):
        gmlp_vec = pl.BlockSpec((None, 1, GMLP_WIDTH), lambda j, m: (layer, 0, 0))
        qkv_groups = [
            _project(xb, [(w_in, layer, T_Q + g, 0), (w_in, layer, T_K + g, 0), (w_in, layer, T_V + g, 0)],
                     1, "qkv", f"proj_qkv_g{g}", tables, dilation=dil)
            for g, (_, dil) in enumerate(DILATED_PATTERNS)]
        h = _project(xb, [(w_in, layer, T_CONV, 1), (w_in, layer, T_CONV + CONV_WIDTH // W_TILE, 1)],
                     CONV_WIDTH // W_TILE, "glu", "proj_glu")
        hgmlp = _project(xb, [(w_in, layer, T_GMLP + k, 0) for k in range(2 * GMLP_WIDTH // W_TILE)],
                         1, "gelu_sgu", "proj_gmlp", tm=SGU_TM, extra=[
                             (w_spatial, pl.BlockSpec((None, GMLP_GROUPS, GMLP_CHUNK, GMLP_CHUNK),
                                                      lambda j, m: (layer, 0, 0, 0))),
                             (b_spatial_t, pl.BlockSpec((None, GMLP_CHUNK, GMLP_GROUPS),
                                                        lambda j, m: (layer, 0, 0))),
                             (gmlp_ln_g.reshape(DEPTH, 1, GMLP_WIDTH), gmlp_vec),
                             (gmlp_ln_b.reshape(DEPTH, 1, GMLP_WIDTH), gmlp_vec)])
        gates = _project(xb, [(w_in, layer, T_GATES, 2), (w_in, layer, T_GATES + 1, 2)],
                         3 * D_MODEL // (2 * W_TILE), "sigmoid", "proj_gates")
        conv_cm = _conformer_conv(h, conv_dw, conv_dw_b, layer)
        attn = _attention(qkv_groups)
        xf, xb = _merge(attn, conv_cm, hgmlp, gates, xf, wa, wc, wg, wo,
                        conv_ln_g, conv_ln_b, ln1_g, ln1_b, layer)
        hid = _project(xb, [(w_ffn_gate, layer, 0, 1), (w_ffn_up, layer, 0, 1)],
                       FFN_HIDDEN // W_TILE, "swiglu", "ffn_up")
        xf, xb = _ffn_down(hid, xf, wd, ln2_g, ln2_b, layer)
    return xf.reshape(BATCH, SEQ, D_MODEL)
```

```python
import functools
import math

import jax
import jax.numpy as jnp
from jax import lax
from jax.experimental import pallas as pl
from jax.experimental.pallas import tpu as pltpu

D_MODEL = 2048
BATCH = 4
SEQ = 2048
DEPTH = 2
TOKENS = BATCH * SEQ
HEAD_DIM = 128
ATTN_SLOTS = 4
DILATED_PATTERNS = ((128, 1), (512, 4), (2048, 16))
N_GROUPS = len(DILATED_PATTERNS)
N_ATTN_HEADS = N_GROUPS * ATTN_SLOTS
ATTN_WIDTH = N_ATTN_HEADS * HEAD_DIM
ATTN_OUT = ATTN_SLOTS * HEAD_DIM
ATTN_BAND = 128
ROPE_THETA = 500000.0
ROPE_DIM = HEAD_DIM // 4
ROPE_HALF = ROPE_DIM // 2
CONV_WIDTH = D_MODEL // 2
CONV_KERNEL = 31
GMLP_WIDTH = D_MODEL // 2
GMLP_CHUNK = 128
GMLP_GROUPS = 8
GMLP_GROUP_CH = GMLP_WIDTH // GMLP_GROUPS
FFN_HIDDEN = -(-8 * D_MODEL // (3 * 256)) * 256
DEEPNORM_ALPHA = (2 * DEPTH) ** 0.25
LN_EPS = 1e-5
NEG_INF = -1e30
IN_WIDTH = 3 * ATTN_WIDTH + 2 * CONV_WIDTH + 2 * GMLP_WIDTH + 3 * D_MODEL

LANES = 128
SUBLANES = 8
V7X_VMEM_LIMIT_BYTES = 56 * 1024 * 1024

W_TILE = 512
PROJ_TM = 1024
T_Q, T_K, T_V = 0, ATTN_WIDTH // W_TILE, 2 * ATTN_WIDTH // W_TILE
T_CONV = 3 * ATTN_WIDTH // W_TILE
T_GMLP = T_CONV + 2 * CONV_WIDTH // W_TILE
T_GATES = T_GMLP + 2 * GMLP_WIDTH // W_TILE
ATTN_TQ = 128
CONV_TS = 256
CONV_HALO = 32
CONV_ROWS = 128
SGU_TM = 512
CAST_ROWS = 512
MERGE_TM = 256
DOWN_TM = 256
assert W_TILE == ATTN_SLOTS * HEAD_DIM and SEQ % PROJ_TM == 0
assert all(window // dil == ATTN_BAND and PROJ_TM % dil == 0 for window, dil in DILATED_PATTERNS)

_F32 = jnp.float32
_BF16 = jnp.bfloat16


def _params(semantics):
    return pltpu.CompilerParams(dimension_semantics=semantics,
                                vmem_limit_bytes=V7X_VMEM_LIMIT_BYTES)


def _layer_norm(x, g, b):
    mu = jnp.mean(x, axis=-1, keepdims=True)
    xc = x - mu
    var = jnp.mean(xc * xc, axis=-1, keepdims=True)
    return xc * lax.rsqrt(var + LN_EPS) * g + b


def _silu(x):
    h = 0.5 * x
    return h + h * jnp.tanh(h)


def _gelu(x):
    return 0.5 * x * (1.0 + lax.erf(x * math.sqrt(0.5)))


def _rope_table_kernel(cos_ref, sin_lo_ref, sin_hi_ref):
    pos = lax.broadcasted_iota(jnp.int32, (SEQ, LANES), 0).astype(_F32)
    lane = lax.broadcasted_iota(jnp.int32, (SEQ, LANES), 1)
    fidx = (lane & (ROPE_HALF - 1)).astype(_F32)
    inv_freq = jnp.exp(fidx * (-math.log(ROPE_THETA) / ROPE_HALF))
    ang = pos * inv_freq
    c = jnp.cos(ang)
    s = jnp.sin(ang)
    cos_ref[...] = jnp.where(lane < ROPE_DIM, c, 1.0)
    sin_lo_ref[...] = jnp.where(lane < ROPE_HALF, -s, 0.0)
    sin_hi_ref[...] = jnp.where((lane >= ROPE_HALF) & (lane < ROPE_DIM), s, 0.0)


def _rope_tables():
    shp = jax.ShapeDtypeStruct((SEQ, LANES), _F32)
    return pl.pallas_call(_rope_table_kernel, out_shape=(shp, shp, shp), name="rope_tables")()


def _proj_kernel(*refs, n_w, epilogue, dilation=1):
    x_ref = refs[0]
    w_refs = refs[1:1 + n_w]
    n_scratch = 2 if epilogue == "qkv" and dilation > 1 else 1
    extra = refs[1 + n_w:-1 - n_scratch]
    o_ref, wbf_ref = refs[-1 - n_scratch], refs[-n_scratch]

    halved = {"sigmoid": range(n_w), "glu": (1,), "swiglu": (0,)}.get(epilogue, ())

    @pl.when(pl.program_id(1) == 0)
    def _():
        for k, w_ref in enumerate(w_refs):
            w = w_ref[...] * 0.5 if k in halved else w_ref[...]
            wbf_ref[:, k * W_TILE:(k + 1) * W_TILE] = w.astype(_BF16)

    acc = jnp.dot(x_ref[...], wbf_ref[...], preferred_element_type=_F32)
    if epilogue == "gelu_sgu":
        ws_ref, bs_ref, g_ref, b_ref = extra
        z = _gelu(acc)
        vn = _layer_norm(z[:, GMLP_WIDTH:], g_ref[...], b_ref[...]).astype(_BF16)
        chunks = x_ref.shape[0] // GMLP_CHUNK
        row = lax.broadcasted_iota(jnp.int32, (GMLP_CHUNK, GMLP_CHUNK), 0)
        col = lax.broadcasted_iota(jnp.int32, (GMLP_CHUNK, GMLP_CHUNK), 1)
        for g in range(GMLP_GROUPS):
            cols = slice(g * GMLP_GROUP_CH, (g + 1) * GMLP_GROUP_CH)
            wm = jnp.where(col <= row, ws_ref[g], 0.0).astype(_BF16)
            rhs = jnp.concatenate(
                [vn[n * GMLP_CHUNK:(n + 1) * GMLP_CHUNK, cols] for n in range(chunks)], axis=1)
            s = jnp.dot(wm, rhs, preferred_element_type=_F32) + bs_ref[:, g:g + 1]
            for n in range(chunks):
                rows = slice(n * GMLP_CHUNK, (n + 1) * GMLP_CHUNK)
                o_ref[rows, cols] = (z[rows, cols] * s[:, n * GMLP_GROUP_CH:(n + 1) * GMLP_GROUP_CH]
                                     ).astype(o_ref.dtype)
        return
    elif epilogue == "sigmoid":
        out = 0.5 * jnp.tanh(acc) + 0.5
    elif epilogue == "glu":
        out = acc[:, :W_TILE] * (0.5 * jnp.tanh(acc[:, W_TILE:]) + 0.5)
        for k in range(W_TILE // LANES):
            o_ref[k] = out[:, k * LANES:(k + 1) * LANES].astype(o_ref.dtype)
        return
    elif epilogue == "swiglu":
        h = acc[:, :W_TILE]
        out = (h + h * jnp.tanh(h)) * acc[:, W_TILE:]
    elif epilogue == "qkv":
        cos_ref, sin_lo_ref, sin_hi_ref = extra
        c, s_lo, s_hi = cos_ref[...], sin_lo_ref[...], sin_hi_ref[...]
        rows_per_class = PROJ_TM // dilation
        for h in range(n_w * W_TILE // HEAD_DIM):
            cols = slice(h * HEAD_DIM, (h + 1) * HEAD_DIM)
            r = acc[:, cols]
            if h < 2 * ATTN_SLOTS:
                r = (r * c + pltpu.roll(r, HEAD_DIM - ROPE_HALF, 1) * s_lo
                     + pltpu.roll(r, ROPE_HALF, 1) * s_hi)
            if h < ATTN_SLOTS:
                r = r * HEAD_DIM ** -0.5
            if dilation == 1:
                o_ref[:, cols] = r.astype(o_ref.dtype)
            else:
                stage_ref = refs[-1]
                stage_ref[h] = r
                for cls in range(dilation):
                    o_ref[cls * rows_per_class:(cls + 1) * rows_per_class, cols] = (
                        stage_ref[h, pl.ds(cls, rows_per_class, stride=dilation), :].astype(o_ref.dtype))
        return
    else:
        raise ValueError(epilogue)
    o_ref[...] = out.astype(o_ref.dtype)


def _project(xb, weights, n_steps, epilogue, name, extra=(), dilation=1, tm=PROJ_TM):
    n_w = len(weights)
    w_mode = dict(pipeline_mode=pl.Buffered(1)) if n_steps == 1 else {}
    scratch = [pltpu.VMEM((D_MODEL, n_w * W_TILE), _BF16)]
    if epilogue == "qkv" and dilation > 1:
        assert tm == PROJ_TM
        scratch.append(pltpu.VMEM((n_w * W_TILE // HEAD_DIM, PROJ_TM, HEAD_DIM), _F32))
    out_tn = {"glu": W_TILE, "swiglu": W_TILE, "gelu_sgu": GMLP_WIDTH}.get(epilogue, n_w * W_TILE)
    if epilogue == "glu":
        chunks = out_tn // LANES
        out_spec = pl.BlockSpec((chunks, tm, LANES), lambda j, m: (j, m, 0))
        out_shape = jax.ShapeDtypeStruct((n_steps * chunks, TOKENS, LANES), _BF16)
    else:
        out_spec = pl.BlockSpec((tm, out_tn), lambda j, m: (m, j))
        out_shape = jax.ShapeDtypeStruct((TOKENS, n_steps * out_tn), _BF16)
    w_specs = [pl.BlockSpec((None, D_MODEL, W_TILE),
                            lambda j, m, layer=layer, first=first, stride=stride:
                            (layer, 0, first + stride * j), **w_mode)
               for _, layer, first, stride in weights]
    return pl.pallas_call(
        functools.partial(_proj_kernel, n_w=n_w, epilogue=epilogue, dilation=dilation),
        grid=(n_steps, TOKENS // tm),
        in_specs=[pl.BlockSpec((tm, D_MODEL), lambda j, m: (m, 0))] + w_specs + [s for _, s in extra],
        out_specs=out_spec,
        out_shape=out_shape,
        scratch_shapes=scratch,
        compiler_params=_params(("arbitrary", "arbitrary")),
        name=name,
    )(xb, *[w for w, _, _, _ in weights], *[a for a, _ in extra])


def _attn_kernel(q0, k0, v0, q1, k1, v1, q2, k2, v2, o_ref, *scratch):
    groups = ((q0, k0, v0), (q1, k1, v1), (q2, k2, v2))
    stats = scratch[:3 * N_GROUPS]
    vcs = scratch[3 * N_GROUPS:]
    qi = lax.broadcasted_iota(jnp.int32, (ATTN_TQ, 2 * ATTN_BAND), 0)
    kj = lax.broadcasted_iota(jnp.int32, (ATTN_TQ, 2 * ATTN_BAND), 1)
    band_mask = (kj >= qi) & (kj <= qi + ATTN_BAND)
    causal_mask = (lax.broadcasted_iota(jnp.int32, (ATTN_TQ, ATTN_BAND), 1)
                   <= lax.broadcasted_iota(jnp.int32, (ATTN_TQ, ATTN_BAND), 0))

    def class_rows(ref, dil, cls, start, count):
        per_tile = PROJ_TM // dil
        pieces, pos = [], start
        while pos < start + count:
            tile, off = divmod(pos, per_tile)
            take = min(per_tile - off, start + count - pos)
            row = tile * PROJ_TM + cls * per_tile + off
            pieces.append(ref[row:row + take, :])
            pos += take
        return pieces[0] if len(pieces) == 1 else jnp.concatenate(pieces, axis=0)

    for g, (window, dil) in enumerate(DILATED_PATTERNS):
        q_ref, k_ref, v_ref = groups[g]
        m_ref, den_ref, pv_ref = stats[3 * g:3 * g + 3]
        vc_ref = vcs[g]
        vc_ref[:, HEAD_DIM:] = jnp.ones((SEQ, HEAD_DIM), _BF16)
        vc_ref[:, :HEAD_DIM] = v_ref[...]
        length = SEQ // dil
        for c in range(dil):
            for i in range(length // ATTN_TQ):
                row0 = i * ATTN_TQ
                k_lo = row0 if i == 0 else row0 - ATTN_BAND
                span = row0 + ATTN_TQ - k_lo
                q = class_rows(q_ref, dil, c, row0, ATTN_TQ)
                k = class_rows(k_ref, dil, c, k_lo, span)
                s = lax.dot_general(q, k, (((1,), (1,)), ((), ())), preferred_element_type=_F32)
                s = jnp.where(causal_mask if i == 0 else band_mask, s, NEG_INF)
                m = jnp.max(s, axis=-1, keepdims=True)
                e = jnp.exp(s - m).astype(_BF16)
                r = jnp.dot(e, class_rows(vc_ref, dil, c, k_lo, span), preferred_element_type=_F32)
                if dil == 1:
                    tok = pl.ds(row0, ATTN_TQ)
                else:
                    tok = pl.ds(c + dil * i * ATTN_TQ, ATTN_TQ, stride=dil)
                m_ref[tok, :] = jnp.broadcast_to(m, (ATTN_TQ, HEAD_DIM))
                den_ref[tok, :] = r[:, HEAD_DIM:]
                pv_ref[tok, :] = r[:, :HEAD_DIM]

    rows_per_step = 2 * ATTN_TQ
    for t in range(SEQ // rows_per_step):
        rows = slice(t * rows_per_step, (t + 1) * rows_per_step)
        ms = [stats[3 * g][rows, :] for g in range(N_GROUPS)]
        m_all = functools.reduce(jnp.maximum, ms)
        num = jnp.zeros((rows_per_step, HEAD_DIM), _F32)
        z = jnp.zeros((rows_per_step, HEAD_DIM), _F32)
        for g in range(N_GROUPS):
            w = jnp.exp(ms[g] - m_all)
            z = z + w * stats[3 * g + 1][rows, :]
            num = num + w * stats[3 * g + 2][rows, :]
        o_ref[rows, :] = (num / z).astype(o_ref.dtype)


def _attention(qkv_groups):
    in_specs, operands = [], []
    for qkv in qkv_groups:
        for part in range(3):
            in_specs.append(pl.BlockSpec(
                (SEQ, HEAD_DIM), lambda b, s, part=part: (b, part * ATTN_SLOTS + s)))
            operands.append(qkv)
    tok_f32 = pltpu.VMEM((SEQ, HEAD_DIM), _F32)
    return pl.pallas_call(
        _attn_kernel,
        grid=(BATCH, ATTN_SLOTS),
        in_specs=in_specs,
        out_specs=pl.BlockSpec((SEQ, HEAD_DIM), lambda b, s: (b, s)),
        out_shape=jax.ShapeDtypeStruct((TOKENS, ATTN_OUT), _BF16),
        scratch_shapes=([tok_f32] * (3 * N_GROUPS)
                        + [pltpu.VMEM((SEQ, 2 * HEAD_DIM), _BF16)] * N_GROUPS),
        compiler_params=_params(("arbitrary", "arbitrary")),
        name="dilated_attention",
    )(*operands)


def _conv_block(c, base, cw_ref, cb_ref, cv_ref, hp_ref):
    off = CONV_HALO - (CONV_KERNEL - 1)
    vregs_per_block = CONV_ROWS // SUBLANES
    accs = [jnp.broadcast_to(cb_ref[c], (SUBLANES, LANES))] * vregs_per_block
    for j in range(CONV_KERNEL):
        wj = jnp.broadcast_to(cw_ref[c, j:j + 1, :], (SUBLANES, LANES))
        for v in range(vregs_per_block):
            accs[v] = accs[v] + wj * hp_ref[c, pl.ds(base + off + j + v * SUBLANES, SUBLANES), :]
    for v in range(vregs_per_block):
        cv_ref[c, pl.ds(base + v * SUBLANES, SUBLANES), :] = accs[v]


def _conv_kernel(h_ref, ph_ref, cw_ref, cb_ref, cv_ref, hp_ref):
    first = pl.program_id(1) == 0
    blocks_per_chunk = CONV_TS // CONV_ROWS

    def fill(c, carry):
        hp_ref[c, 0:CONV_HALO, :] = jnp.where(first, 0.0, ph_ref[c].astype(_F32))
        hp_ref[c, CONV_HALO:, :] = h_ref[c].astype(_F32)
        return carry

    lax.fori_loop(0, CONV_WIDTH // LANES, fill, 0)

    def block(i, carry):
        _conv_block(i // blocks_per_chunk, (i % blocks_per_chunk) * CONV_ROWS,
                    cw_ref, cb_ref, cv_ref, hp_ref)
        return carry

    lax.fori_loop(0, (CONV_WIDTH // LANES) * blocks_per_chunk, block, 0)


def _conformer_conv(h_cm, w_dw, b_dw, layer):
    n_chunks = CONV_WIDTH // LANES
    tiles = SEQ // CONV_TS
    halo_per_tile = CONV_TS // CONV_HALO
    w_chunks = jnp.swapaxes(w_dw.reshape(DEPTH, CONV_KERNEL, n_chunks, LANES), 1, 2)
    b_chunks = b_dw.reshape(DEPTH, n_chunks, 1, LANES)
    return pl.pallas_call(
        _conv_kernel,
        grid=(BATCH, tiles),
        in_specs=[pl.BlockSpec((n_chunks, CONV_TS, LANES), lambda b, i: (0, b * tiles + i, 0)),
                  pl.BlockSpec((n_chunks, CONV_HALO, LANES),
                               lambda b, i: (0, jnp.maximum((b * tiles + i) * halo_per_tile - 1, 0), 0)),
                  pl.BlockSpec((None, n_chunks, CONV_KERNEL, LANES), lambda b, i: (layer, 0, 0, 0)),
                  pl.BlockSpec((None, n_chunks, 1, LANES), lambda b, i: (layer, 0, 0, 0))],
        out_specs=pl.BlockSpec((n_chunks, CONV_TS, LANES), lambda b, i: (0, b * tiles + i, 0)),
        out_shape=jax.ShapeDtypeStruct((n_chunks, TOKENS, LANES), _F32),
        scratch_shapes=[pltpu.VMEM((n_chunks, CONV_HALO + CONV_TS, LANES), _F32)],
        compiler_params=_params(("arbitrary", "arbitrary")),
        name="conformer_conv",
    )(h_cm, h_cm, w_chunks, b_chunks)


def _cast_kernel(w_ref, o_ref):
    o_ref[...] = w_ref[...].astype(o_ref.dtype)


def _to_bf16(w):
    depth, k, n = w.shape
    spec = pl.BlockSpec((None, CAST_ROWS, n), lambda l, i: (l, i, 0))
    return pl.pallas_call(
        _cast_kernel,
        grid=(depth, k // CAST_ROWS),
        in_specs=[spec],
        out_specs=spec,
        out_shape=jax.ShapeDtypeStruct(w.shape, _BF16),
        compiler_params=_params(("arbitrary", "arbitrary")),
        name="cast_weights",
    )(w)


def _merge_kernel(attn_ref, cv_ref, hg_ref, ga_ref, gc_ref, gm_ref, x_ref,
                  wa_ref, wc_ref, wg_ref, wo_ref, cg_ref, cb_ref, g_ref, b_ref, xo_ref, xob_ref):
    conv = jnp.concatenate([cv_ref[c] for c in range(CONV_WIDTH // LANES)], axis=1)
    hc = _layer_norm(conv, cg_ref[...], cb_ref[...])
    hc = _silu(hc).astype(_BF16)
    ya = jnp.dot(attn_ref[...], wa_ref[...], preferred_element_type=_F32)
    merged = ga_ref[...].astype(_F32) * ya
    yc = jnp.dot(hc, wc_ref[...], preferred_element_type=_F32)
    merged = merged + gc_ref[...].astype(_F32) * yc
    yg = jnp.dot(hg_ref[...], wg_ref[...], preferred_element_type=_F32)
    merged = merged + gm_ref[...].astype(_F32) * yg
    y = jnp.dot(merged.astype(_BF16), wo_ref[...], preferred_element_type=_F32)
    out = _layer_norm(DEEPNORM_ALPHA * x_ref[...] + y, g_ref[...], b_ref[...])
    xo_ref[...] = out
    xob_ref[...] = out.astype(_BF16)


def _resident(shape, layer):
    return pl.BlockSpec((None,) + shape, lambda i: (layer, 0, 0), pipeline_mode=pl.Buffered(1))


def _merge(attn, conv_cm, hgmlp, gates, x, wa, wc, wg, wo, conv_ln_g, conv_ln_b, ln_g, ln_b, layer):
    tm = MERGE_TM
    row = lambda width: pl.BlockSpec((tm, width), lambda i: (i, 0))
    gate = lambda k: pl.BlockSpec((tm, D_MODEL), lambda i: (i, k))
    vec = lambda width: pl.BlockSpec((None, 1, width), lambda i: (layer, 0, 0))
    out_spec = pl.BlockSpec((tm, D_MODEL), lambda i: (i, 0))
    return pl.pallas_call(
        _merge_kernel,
        grid=(TOKENS // tm,),
        in_specs=[row(ATTN_OUT),
                  pl.BlockSpec((CONV_WIDTH // LANES, tm, LANES), lambda i: (0, i, 0)),
                  row(GMLP_WIDTH), gate(0), gate(1), gate(2), row(D_MODEL),
                  _resident((ATTN_OUT, D_MODEL), layer), _resident((CONV_WIDTH, D_MODEL), layer),
                  _resident((GMLP_WIDTH, D_MODEL), layer), _resident((D_MODEL, D_MODEL), layer),
                  vec(CONV_WIDTH), vec(CONV_WIDTH), vec(D_MODEL), vec(D_MODEL)],
        out_specs=(out_spec, out_spec),
        out_shape=(jax.ShapeDtypeStruct((TOKENS, D_MODEL), _F32),
                   jax.ShapeDtypeStruct((TOKENS, D_MODEL), _BF16)),
        compiler_params=_params(("arbitrary",)),
        name="merge_out_ln1",
    )(attn, conv_cm, hgmlp, gates, gates, gates, x, wa, wc, wg, wo,
      conv_ln_g.reshape(DEPTH, 1, CONV_WIDTH), conv_ln_b.reshape(DEPTH, 1, CONV_WIDTH),
      ln_g.reshape(DEPTH, 1, D_MODEL), ln_b.reshape(DEPTH, 1, D_MODEL))


def _ffn_down_kernel(h_ref, x_ref, w_ref, g_ref, b_ref, xo_ref, xob_ref):
    f = jnp.dot(h_ref[...], w_ref[...], preferred_element_type=_F32)
    out = _layer_norm(DEEPNORM_ALPHA * x_ref[...] + f, g_ref[...], b_ref[...])
    xo_ref[...] = out
    xob_ref[...] = out.astype(_BF16)


def _ffn_down(h, x, w_down, ln_g, ln_b, layer):
    tm = DOWN_TM
    vec = pl.BlockSpec((None, 1, D_MODEL), lambda i: (layer, 0, 0))
    out_spec = pl.BlockSpec((tm, D_MODEL), lambda i: (i, 0))
    return pl.pallas_call(
        _ffn_down_kernel,
        grid=(TOKENS // tm,),
        in_specs=[pl.BlockSpec((tm, FFN_HIDDEN), lambda i: (i, 0)),
                  pl.BlockSpec((tm, D_MODEL), lambda i: (i, 0)),
                  _resident((FFN_HIDDEN, D_MODEL), layer), vec, vec],
        out_specs=(out_spec, out_spec),
        out_shape=(jax.ShapeDtypeStruct((TOKENS, D_MODEL), _F32),
                   jax.ShapeDtypeStruct((TOKENS, D_MODEL), _BF16)),
        compiler_params=_params(("arbitrary",)),
        name="ffn_down_ln2",
    )(h, x, w_down, ln_g.reshape(DEPTH, 1, D_MODEL), ln_b.reshape(DEPTH, 1, D_MODEL))


def kernel(x, w_in, w_attn_proj, conv_dw, conv_dw_b, conv_ln_g, conv_ln_b, w_conv_proj, gmlp_ln_g, gmlp_ln_b, w_spatial, b_spatial, w_gmlp_proj, w_out, ln1_g, ln1_b, w_ffn_gate, w_ffn_up, w_ffn_down, ln2_g, ln2_b):
    assert x.shape == (BATCH, SEQ, D_MODEL) and w_in.shape == (DEPTH, D_MODEL, IN_WIDTH)
    seq_tiles = SEQ // PROJ_TM
    table_spec = pl.BlockSpec((PROJ_TM, LANES), lambda j, m: (m % seq_tiles, 0))
    tables = [(t, table_spec) for t in _rope_tables()]
    xf = x.reshape(TOKENS, D_MODEL)
    xb = xf.astype(_BF16)
    b_spatial_t = jnp.swapaxes(b_spatial, 1, 2)
    wa, wc, wg, wo, wd = [_to_bf16(w) for w in (w_attn_proj, w_conv_proj, w_gmlp_proj, w_out, w_ffn_down)]
    for layer in range(DEPTH):
        gmlp_vec = pl.BlockSpec((None, 1, GMLP_WIDTH), lambda j, m: (layer, 0, 0))
        qkv_groups = [
            _project(xb, [(w_in, layer, T_Q + g, 0), (w_in, layer, T_K + g, 0), (w_in, layer, T_V + g, 0)],
                     1, "qkv", f"proj_qkv_g{g}", tables, dilation=dil)
            for g, (_, dil) in enumerate(DILATED_PATTERNS)]
        h = _project(xb, [(w_in, layer, T_CONV, 1), (w_in, layer, T_CONV + CONV_WIDTH // W_TILE, 1)],
                     CONV_WIDTH // W_TILE, "glu", "proj_glu")
        hgmlp = _project(xb, [(w_in, layer, T_GMLP + k, 0) for k in range(2 * GMLP_WIDTH // W_TILE)],
                         1, "gelu_sgu", "proj_gmlp", tm=SGU_TM, extra=[
                             (w_spatial, pl.BlockSpec((None, GMLP_GROUPS, GMLP_CHUNK, GMLP_CHUNK),
                                                      lambda j, m: (layer, 0, 0, 0))),
                             (b_spatial_t, pl.BlockSpec((None, GMLP_CHUNK, GMLP_GROUPS),
                                                        lambda j, m: (layer, 0, 0))),
                             (gmlp_ln_g.reshape(DEPTH, 1, GMLP_WIDTH), gmlp_vec),
                             (gmlp_ln_b.reshape(DEPTH, 1, GMLP_WIDTH), gmlp_vec)])
        gates = _project(xb, [(w_in, layer, T_GATES, 2), (w_in, layer, T_GATES + 1, 2)],
                         3 * D_MODEL // (2 * W_TILE), "sigmoid", "proj_gates")
        conv_cm = _conformer_conv(h, conv_dw, conv_dw_b, layer)
        attn = _attention(qkv_groups)
        xf, xb = _merge(attn, conv_cm, hgmlp, gates, xf, wa, wc, wg, wo,
                        conv_ln_g, conv_ln_b, ln1_g, ln1_b, layer)
        hid = _project(xb, [(w_ffn_gate, layer, 0, 1), (w_ffn_up, layer, 0, 1)],
                       FFN_HIDDEN // W_TILE, "swiglu", "ffn_up")
        xf, xb = _ffn_down(hid, xf, wd, ln2_g, ln2_b, layer)
    return xf.reshape(BATCH, SEQ, D_MODEL)
```

```python
import functools
import math

import jax
import jax.numpy as jnp
from jax import lax
from jax.experimental import pallas as pl
from jax.experimental.pallas import tpu as pltpu

D_MODEL = 2048
BATCH = 4
SEQ = 2048
DEPTH = 2
TOKENS = BATCH * SEQ
HEAD_DIM = 128
ATTN_SLOTS = 4
DILATED_PATTERNS = ((128, 1), (512, 4), (2048, 16))
N_GROUPS = len(DILATED_PATTERNS)
N_ATTN_HEADS = N_GROUPS * ATTN_SLOTS
ATTN_WIDTH = N_ATTN_HEADS * HEAD_DIM
ATTN_OUT = ATTN_SLOTS * HEAD_DIM
ATTN_BAND = 128
ROPE_THETA = 500000.0
ROPE_DIM = HEAD_DIM // 4
ROPE_HALF = ROPE_DIM // 2
CONV_WIDTH = D_MODEL // 2
CONV_KERNEL = 31
GMLP_WIDTH = D_MODEL // 2
GMLP_CHUNK = 128
GMLP_GROUPS = 8
GMLP_GROUP_CH = GMLP_WIDTH // GMLP_GROUPS
FFN_HIDDEN = -(-8 * D_MODEL // (3 * 256)) * 256
DEEPNORM_ALPHA = (2 * DEPTH) ** 0.25
LN_EPS = 1e-5
NEG_INF = -1e30
IN_WIDTH = 3 * ATTN_WIDTH + 2 * CONV_WIDTH + 2 * GMLP_WIDTH + 3 * D_MODEL

LANES = 128
SUBLANES = 8
V7X_VMEM_LIMIT_BYTES = 56 * 1024 * 1024

W_TILE = 512
PROJ_TM = 1024
T_Q, T_K, T_V = 0, ATTN_WIDTH // W_TILE, 2 * ATTN_WIDTH // W_TILE
T_CONV = 3 * ATTN_WIDTH // W_TILE
T_GMLP = T_CONV + 2 * CONV_WIDTH // W_TILE
T_GATES = T_GMLP + 2 * GMLP_WIDTH // W_TILE
ATTN_TQ = 128
REGROUP_STRIDE = 4
CONV_TS = 256
CONV_HALO = 32
CONV_ROWS = 128
SGU_TM = 512
CAST_ROWS = 512
MERGE_TM = 256
DOWN_TM = 256
assert W_TILE == ATTN_SLOTS * HEAD_DIM and SEQ % PROJ_TM == 0
assert all(window // dil == ATTN_BAND and PROJ_TM % dil == 0 for window, dil in DILATED_PATTERNS)

_F32 = jnp.float32
_BF16 = jnp.bfloat16


def _params(semantics):
    return pltpu.CompilerParams(dimension_semantics=semantics,
                                vmem_limit_bytes=V7X_VMEM_LIMIT_BYTES)


def _layer_norm(x, g, b):
    mu = jnp.mean(x, axis=-1, keepdims=True)
    xc = x - mu
    var = jnp.mean(xc * xc, axis=-1, keepdims=True)
    return xc * lax.rsqrt(var + LN_EPS) * g + b


def _silu(x):
    h = 0.5 * x
    return h + h * jnp.tanh(h)


def _gelu(x):
    return 0.5 * x * (1.0 + lax.erf(x * math.sqrt(0.5)))


def _rope_table_kernel(cos_ref, sin_lo_ref, sin_hi_ref):
    pos = lax.broadcasted_iota(jnp.int32, (SEQ, LANES), 0).astype(_F32)
    lane = lax.broadcasted_iota(jnp.int32, (SEQ, LANES), 1)
    fidx = (lane & (ROPE_HALF - 1)).astype(_F32)
    inv_freq = jnp.exp(fidx * (-math.log(ROPE_THETA) / ROPE_HALF))
    ang = pos * inv_freq
    c = jnp.cos(ang)
    s = jnp.sin(ang)
    cos_ref[...] = jnp.where(lane < ROPE_DIM, c, 1.0)
    sin_lo_ref[...] = jnp.where(lane < ROPE_HALF, -s, 0.0)
    sin_hi_ref[...] = jnp.where((lane >= ROPE_HALF) & (lane < ROPE_DIM), s, 0.0)


def _rope_tables():
    shp = jax.ShapeDtypeStruct((SEQ, LANES), _F32)
    return pl.pallas_call(_rope_table_kernel, out_shape=(shp, shp, shp), name="rope_tables")()


def _regroup_stages(dilation):
    return 0 if dilation == 1 else 1 if dilation <= REGROUP_STRIDE else 2


def _proj_kernel(*refs, n_w, epilogue, dilation=1, emit_x_bf16=False):
    x_ref = refs[0]
    w_refs = refs[1:1 + n_w]
    n_scratch = 1 + _regroup_stages(dilation) if epilogue == "qkv" else 1
    n_out = 2 if emit_x_bf16 else 1
    extra = refs[1 + n_w:-n_out - n_scratch]
    o_ref, wbf_ref = refs[-n_out - n_scratch], refs[-n_scratch]

    halved = {"sigmoid": range(n_w), "glu": (1,), "swiglu": (0,)}.get(epilogue, ())

    @pl.when(pl.program_id(1) == 0)
    def _():
        for k, w_ref in enumerate(w_refs):
            w = w_ref[...] * 0.5 if k in halved else w_ref[...]
            wbf_ref[:, k * W_TILE:(k + 1) * W_TILE] = w.astype(_BF16)

    if emit_x_bf16:
        x = x_ref[...].astype(_BF16)
        refs[-n_scratch - 1][...] = x
    else:
        x = x_ref[...]
    acc = jnp.dot(x, wbf_ref[...], preferred_element_type=_F32)
    if epilogue == "gelu_sgu":
        ws_ref, bs_ref, g_ref, b_ref = extra
        z = _gelu(acc)
        vn = _layer_norm(z[:, GMLP_WIDTH:], g_ref[...], b_ref[...]).astype(_BF16)
        chunks = x_ref.shape[0] // GMLP_CHUNK
        row = lax.broadcasted_iota(jnp.int32, (GMLP_CHUNK, GMLP_CHUNK), 0)
        col = lax.broadcasted_iota(jnp.int32, (GMLP_CHUNK, GMLP_CHUNK), 1)
        for g in range(GMLP_GROUPS):
            cols = slice(g * GMLP_GROUP_CH, (g + 1) * GMLP_GROUP_CH)
            wm = jnp.where(col <= row, ws_ref[g], 0.0).astype(_BF16)
            rhs = jnp.concatenate(
                [vn[n * GMLP_CHUNK:(n + 1) * GMLP_CHUNK, cols] for n in range(chunks)], axis=1)
            s = jnp.dot(wm, rhs, preferred_element_type=_F32) + bs_ref[:, g:g + 1]
            for n in range(chunks):
                rows = slice(n * GMLP_CHUNK, (n + 1) * GMLP_CHUNK)
                o_ref[rows, cols] = (z[rows, cols] * s[:, n * GMLP_GROUP_CH:(n + 1) * GMLP_GROUP_CH]
                                     ).astype(o_ref.dtype)
        return
    elif epilogue == "sigmoid":
        out = 0.5 * jnp.tanh(acc) + 0.5
    elif epilogue == "glu":
        out = acc[:, :W_TILE] * (0.5 * jnp.tanh(acc[:, W_TILE:]) + 0.5)
        for k in range(W_TILE // LANES):
            o_ref[k] = out[:, k * LANES:(k + 1) * LANES].astype(o_ref.dtype)
        return
    elif epilogue == "swiglu":
        h = acc[:, :W_TILE]
        out = (h + h * jnp.tanh(h)) * acc[:, W_TILE:]
    elif epilogue == "qkv":
        cos_ref, sin_lo_ref, sin_hi_ref = extra
        c, s_lo, s_hi = cos_ref[...], sin_lo_ref[...], sin_hi_ref[...]
        rows_per_class = PROJ_TM // dilation
        for h in range(n_w * W_TILE // HEAD_DIM):
            cols = slice(h * HEAD_DIM, (h + 1) * HEAD_DIM)
            r = acc[:, cols]
            if h < 2 * ATTN_SLOTS:
                r = (r * c + pltpu.roll(r, HEAD_DIM - ROPE_HALF, 1) * s_lo
                     + pltpu.roll(r, ROPE_HALF, 1) * s_hi)
            if h < ATTN_SLOTS:
                r = r * HEAD_DIM ** -0.5
            if dilation == 1:
                o_ref[:, cols] = r.astype(o_ref.dtype)
            elif dilation <= REGROUP_STRIDE:
                stage_ref = refs[-1]
                stage_ref[h] = r
                for cls in range(dilation):
                    o_ref[cls * rows_per_class:(cls + 1) * rows_per_class, cols] = (
                        stage_ref[h, pl.ds(cls, rows_per_class, stride=dilation), :].astype(o_ref.dtype))
            else:
                stage_ref, stage2_ref = refs[-2], refs[-1]
                stage_ref[h] = r
                outer = dilation // REGROUP_STRIDE
                rows_inner = PROJ_TM // REGROUP_STRIDE
                for c_in in range(REGROUP_STRIDE):
                    stage2_ref[h, c_in * rows_inner:(c_in + 1) * rows_inner, :] = (
                        stage_ref[h, pl.ds(c_in, rows_inner, stride=REGROUP_STRIDE), :])
                for c_in in range(REGROUP_STRIDE):
                    for k in range(outer):
                        cls = c_in + REGROUP_STRIDE * k
                        o_ref[cls * rows_per_class:(cls + 1) * rows_per_class, cols] = (
                            stage2_ref[h, pl.ds(c_in * rows_inner + k, rows_per_class, stride=outer), :]
                            .astype(o_ref.dtype))
        return
    else:
        raise ValueError(epilogue)
    o_ref[...] = out.astype(o_ref.dtype)


def _project(xb, weights, n_steps, epilogue, name, extra=(), dilation=1, tm=PROJ_TM):
    n_w = len(weights)
    w_mode = dict(pipeline_mode=pl.Buffered(1)) if n_steps == 1 else {}
    scratch = [pltpu.VMEM((D_MODEL, n_w * W_TILE), _BF16)]
    if epilogue == "qkv":
        assert tm == PROJ_TM
        scratch += [pltpu.VMEM((n_w * W_TILE // HEAD_DIM, PROJ_TM, HEAD_DIM), _F32)
                    ] * _regroup_stages(dilation)
    out_tn = {"glu": W_TILE, "swiglu": W_TILE, "gelu_sgu": GMLP_WIDTH}.get(epilogue, n_w * W_TILE)
    if epilogue == "glu":
        chunks = out_tn // LANES
        out_spec = pl.BlockSpec((chunks, tm, LANES), lambda j, m: (j, m, 0))
        out_shape = jax.ShapeDtypeStruct((n_steps * chunks, TOKENS, LANES), _BF16)
    else:
        out_spec = pl.BlockSpec((tm, out_tn), lambda j, m: (m, j))
        out_shape = jax.ShapeDtypeStruct((TOKENS, n_steps * out_tn), _BF16)
    w_specs = [pl.BlockSpec((None, D_MODEL, W_TILE),
                            lambda j, m, layer=layer, first=first, stride=stride:
                            (layer, 0, first + stride * j), **w_mode)
               for _, layer, first, stride in weights]
    emit_x_bf16 = xb.dtype != _BF16
    if emit_x_bf16:
        assert n_steps == 1
        out_spec = (out_spec, pl.BlockSpec((tm, D_MODEL), lambda j, m: (m, 0)))
        out_shape = (out_shape, jax.ShapeDtypeStruct((TOKENS, D_MODEL), _BF16))
    return pl.pallas_call(
        functools.partial(_proj_kernel, n_w=n_w, epilogue=epilogue, dilation=dilation,
                          emit_x_bf16=emit_x_bf16),
        grid=(n_steps, TOKENS // tm),
        in_specs=[pl.BlockSpec((tm, D_MODEL), lambda j, m: (m, 0))] + w_specs + [s for _, s in extra],
        out_specs=out_spec,
        out_shape=out_shape,
        scratch_shapes=scratch,
        compiler_params=_params(("arbitrary", "arbitrary")),
        name=name,
    )(xb, *[w for w, _, _, _ in weights], *[a for a, _ in extra])


def _attn_kernel(q0, k0, v0, q1, k1, v1, q2, k2, v2, o_ref, *scratch):
    groups = ((q0, k0, v0), (q1, k1, v1), (q2, k2, v2))
    stats = scratch[:3 * N_GROUPS]
    vcs = scratch[3 * N_GROUPS:]
    qi = lax.broadcasted_iota(jnp.int32, (ATTN_TQ, 2 * ATTN_BAND), 0)
    kj = lax.broadcasted_iota(jnp.int32, (ATTN_TQ, 2 * ATTN_BAND), 1)
    band_mask = (kj >= qi) & (kj <= qi + ATTN_BAND)
    causal_mask = (lax.broadcasted_iota(jnp.int32, (ATTN_TQ, ATTN_BAND), 1)
                   <= lax.broadcasted_iota(jnp.int32, (ATTN_TQ, ATTN_BAND), 0))

    def class_rows(ref, dil, cls, start, count):
        per_tile = PROJ_TM // dil
        pieces, pos = [], start
        while pos < start + count:
            tile, off = divmod(pos, per_tile)
            take = min(per_tile - off, start + count - pos)
            row = tile * PROJ_TM + cls * per_tile + off
            pieces.append(ref[row:row + take, :])
            pos += take
        return pieces[0] if len(pieces) == 1 else jnp.concatenate(pieces, axis=0)

    for g, (window, dil) in enumerate(DILATED_PATTERNS):
        q_ref, k_ref, v_ref = groups[g]
        m_ref, den_ref, pv_ref = stats[3 * g:3 * g + 3]
        vc_ref = vcs[g]
        vc_ref[:, HEAD_DIM:] = jnp.ones((SEQ, HEAD_DIM), _BF16)
        vc_ref[:, :HEAD_DIM] = v_ref[...]
        length = SEQ // dil
        for c in range(dil):
            for i in range(length // ATTN_TQ):
                row0 = i * ATTN_TQ
                k_lo = row0 if i == 0 else row0 - ATTN_BAND
                span = row0 + ATTN_TQ - k_lo
                q = class_rows(q_ref, dil, c, row0, ATTN_TQ)
                k = class_rows(k_ref, dil, c, k_lo, span)
                s = lax.dot_general(q, k, (((1,), (1,)), ((), ())), preferred_element_type=_F32)
                s = jnp.where(causal_mask if i == 0 else band_mask, s, NEG_INF)
                m = jnp.max(s, axis=-1, keepdims=True)
                e = jnp.exp(s - m).astype(_BF16)
                r = jnp.dot(e, class_rows(vc_ref, dil, c, k_lo, span), preferred_element_type=_F32)
                if dil == 1:
                    tok = pl.ds(row0, ATTN_TQ)
                else:
                    tok = pl.ds(c + dil * i * ATTN_TQ, ATTN_TQ, stride=dil)
                m_ref[tok, :] = jnp.broadcast_to(m, (ATTN_TQ, HEAD_DIM))
                den_ref[tok, :] = r[:, HEAD_DIM:]
                pv_ref[tok, :] = r[:, :HEAD_DIM]

    rows_per_step = 2 * ATTN_TQ
    for t in range(SEQ // rows_per_step):
        rows = slice(t * rows_per_step, (t + 1) * rows_per_step)
        ms = [stats[3 * g][rows, :] for g in range(N_GROUPS)]
        m_all = functools.reduce(jnp.maximum, ms)
        num = jnp.zeros((rows_per_step, HEAD_DIM), _F32)
        z = jnp.zeros((rows_per_step, HEAD_DIM), _F32)
        for g in range(N_GROUPS):
            w = jnp.exp(ms[g] - m_all)
            z = z + w * stats[3 * g + 1][rows, :]
            num = num + w * stats[3 * g + 2][rows, :]
        o_ref[rows, :] = (num / z).astype(o_ref.dtype)


def _attention(qkv_groups):
    in_specs, operands = [], []
    for qkv in qkv_groups:
        for part in range(3):
            in_specs.append(pl.BlockSpec(
                (SEQ, HEAD_DIM), lambda b, s, part=part: (b, part * ATTN_SLOTS + s)))
            operands.append(qkv)
    tok_f32 = pltpu.VMEM((SEQ, HEAD_DIM), _F32)
    return pl.pallas_call(
        _attn_kernel,
        grid=(BATCH, ATTN_SLOTS),
        in_specs=in_specs,
        out_specs=pl.BlockSpec((SEQ, HEAD_DIM), lambda b, s: (b, s)),
        out_shape=jax.ShapeDtypeStruct((TOKENS, ATTN_OUT), _BF16),
        scratch_shapes=([tok_f32] * (3 * N_GROUPS)
                        + [pltpu.VMEM((SEQ, 2 * HEAD_DIM), _BF16)] * N_GROUPS),
        compiler_params=_params(("arbitrary", "arbitrary")),
        name="dilated_attention",
    )(*operands)


def _conv_block(c, base, cw_ref, cb_ref, cv_ref, hp_ref):
    off = CONV_HALO - (CONV_KERNEL - 1)
    vregs_per_block = CONV_ROWS // SUBLANES
    accs = [jnp.broadcast_to(cb_ref[c], (SUBLANES, LANES))] * vregs_per_block
    for j in range(CONV_KERNEL):
        wj = jnp.broadcast_to(cw_ref[c, j:j + 1, :], (SUBLANES, LANES))
        for v in range(vregs_per_block):
            accs[v] = accs[v] + wj * hp_ref[c, pl.ds(base + off + j + v * SUBLANES, SUBLANES), :]
    for v in range(vregs_per_block):
        cv_ref[c, pl.ds(base + v * SUBLANES, SUBLANES), :] = accs[v]


def _conv_kernel(h_ref, ph_ref, cw_ref, cb_ref, cv_ref, hp_ref):
    first = pl.program_id(1) == 0
    blocks_per_chunk = CONV_TS // CONV_ROWS

    def fill(c, carry):
        hp_ref[c, 0:CONV_HALO, :] = jnp.where(first, 0.0, ph_ref[c].astype(_F32))
        hp_ref[c, CONV_HALO:, :] = h_ref[c].astype(_F32)
        return carry

    lax.fori_loop(0, CONV_WIDTH // LANES, fill, 0)

    def block(i, carry):
        _conv_block(i // blocks_per_chunk, (i % blocks_per_chunk) * CONV_ROWS,
                    cw_ref, cb_ref, cv_ref, hp_ref)
        return carry

    lax.fori_loop(0, (CONV_WIDTH // LANES) * blocks_per_chunk, block, 0)


def _conformer_conv(h_cm, w_dw, b_dw, layer):
    n_chunks = CONV_WIDTH // LANES
    tiles = SEQ // CONV_TS
    halo_per_tile = CONV_TS // CONV_HALO
    w_chunks = jnp.swapaxes(w_dw.reshape(DEPTH, CONV_KERNEL, n_chunks, LANES), 1, 2)
    b_chunks = b_dw.reshape(DEPTH, n_chunks, 1, LANES)
    return pl.pallas_call(
        _conv_kernel,
        grid=(BATCH, tiles),
        in_specs=[pl.BlockSpec((n_chunks, CONV_TS, LANES), lambda b, i: (0, b * tiles + i, 0)),
                  pl.BlockSpec((n_chunks, CONV_HALO, LANES),
                               lambda b, i: (0, jnp.maximum((b * tiles + i) * halo_per_tile - 1, 0), 0)),
                  pl.BlockSpec((None, n_chunks, CONV_KERNEL, LANES), lambda b, i: (layer, 0, 0, 0)),
                  pl.BlockSpec((None, n_chunks, 1, LANES), lambda b, i: (layer, 0, 0, 0))],
        out_specs=pl.BlockSpec((n_chunks, CONV_TS, LANES), lambda b, i: (0, b * tiles + i, 0)),
        out_shape=jax.ShapeDtypeStruct((n_chunks, TOKENS, LANES), _F32),
        scratch_shapes=[pltpu.VMEM((n_chunks, CONV_HALO + CONV_TS, LANES), _F32)],
        compiler_params=_params(("arbitrary", "arbitrary")),
        name="conformer_conv",
    )(h_cm, h_cm, w_chunks, b_chunks)


def _cast_kernel(w_ref, o_ref):
    o_ref[...] = w_ref[...].astype(o_ref.dtype)


def _to_bf16(w):
    depth, k, n = w.shape
    spec = pl.BlockSpec((None, CAST_ROWS, n), lambda l, i: (l, i, 0))
    return pl.pallas_call(
        _cast_kernel,
        grid=(depth, k // CAST_ROWS),
        in_specs=[spec],
        out_specs=spec,
        out_shape=jax.ShapeDtypeStruct(w.shape, _BF16),
        compiler_params=_params(("arbitrary", "arbitrary")),
        name="cast_weights",
    )(w)


def _merge_kernel(attn_ref, cv_ref, hg_ref, ga_ref, gc_ref, gm_ref, x_ref,
                  wa_ref, wc_ref, wg_ref, wo_ref, cg_ref, cb_ref, g_ref, b_ref, xo_ref, xob_ref):
    conv = jnp.concatenate([cv_ref[c] for c in range(CONV_WIDTH // LANES)], axis=1)
    hc = _layer_norm(conv, cg_ref[...], cb_ref[...])
    hc = _silu(hc).astype(_BF16)
    ya = jnp.dot(attn_ref[...], wa_ref[...], preferred_element_type=_F32)
    merged = ga_ref[...].astype(_F32) * ya
    yc = jnp.dot(hc, wc_ref[...], preferred_element_type=_F32)
    merged = merged + gc_ref[...].astype(_F32) * yc
    yg = jnp.dot(hg_ref[...], wg_ref[...], preferred_element_type=_F32)
    merged = merged + gm_ref[...].astype(_F32) * yg
    y = jnp.dot(merged.astype(_BF16), wo_ref[...], preferred_element_type=_F32)
    out = _layer_norm(DEEPNORM_ALPHA * x_ref[...] + y, g_ref[...], b_ref[...])
    xo_ref[...] = out
    xob_ref[...] = out.astype(_BF16)


def _resident(shape, layer):
    return pl.BlockSpec((None,) + shape, lambda i: (layer, 0, 0), pipeline_mode=pl.Buffered(1))


def _merge(attn, conv_cm, hgmlp, gates, x, wa, wc, wg, wo, conv_ln_g, conv_ln_b, ln_g, ln_b, layer):
    tm = MERGE_TM
    row = lambda width: pl.BlockSpec((tm, width), lambda i: (i, 0))
    gate = lambda k: pl.BlockSpec((tm, D_MODEL), lambda i: (i, k))
    vec = lambda width: pl.BlockSpec((None, 1, width), lambda i: (layer, 0, 0))
    out_spec = pl.BlockSpec((tm, D_MODEL), lambda i: (i, 0))
    return pl.pallas_call(
        _merge_kernel,
        grid=(TOKENS // tm,),
        in_specs=[row(ATTN_OUT),
                  pl.BlockSpec((CONV_WIDTH // LANES, tm, LANES), lambda i: (0, i, 0)),
                  row(GMLP_WIDTH), gate(0), gate(1), gate(2), row(D_MODEL),
                  _resident((ATTN_OUT, D_MODEL), layer), _resident((CONV_WIDTH, D_MODEL), layer),
                  _resident((GMLP_WIDTH, D_MODEL), layer), _resident((D_MODEL, D_MODEL), layer),
                  vec(CONV_WIDTH), vec(CONV_WIDTH), vec(D_MODEL), vec(D_MODEL)],
        out_specs=(out_spec, out_spec),
        out_shape=(jax.ShapeDtypeStruct((TOKENS, D_MODEL), _F32),
                   jax.ShapeDtypeStruct((TOKENS, D_MODEL), _BF16)),
        compiler_params=_params(("arbitrary",)),
        name="merge_out_ln1",
    )(attn, conv_cm, hgmlp, gates, gates, gates, x, wa, wc, wg, wo,
      conv_ln_g.reshape(DEPTH, 1, CONV_WIDTH), conv_ln_b.reshape(DEPTH, 1, CONV_WIDTH),
      ln_g.reshape(DEPTH, 1, D_MODEL), ln_b.reshape(DEPTH, 1, D_MODEL))


def _ffn_down_kernel(h_ref, x_ref, w_ref, g_ref, b_ref, xo_ref, xob_ref):
    f = jnp.dot(h_ref[...], w_ref[...], preferred_element_type=_F32)
    out = _layer_norm(DEEPNORM_ALPHA * x_ref[...] + f, g_ref[...], b_ref[...])
    xo_ref[...] = out
    xob_ref[...] = out.astype(_BF16)


def _ffn_down(h, x, w_down, ln_g, ln_b, layer):
    tm = DOWN_TM
    vec = pl.BlockSpec((None, 1, D_MODEL), lambda i: (layer, 0, 0))
    out_spec = pl.BlockSpec((tm, D_MODEL), lambda i: (i, 0))
    return pl.pallas_call(
        _ffn_down_kernel,
        grid=(TOKENS // tm,),
        in_specs=[pl.BlockSpec((tm, FFN_HIDDEN), lambda i: (i, 0)),
                  pl.BlockSpec((tm, D_MODEL), lambda i: (i, 0)),
                  _resident((FFN_HIDDEN, D_MODEL), layer), vec, vec],
        out_specs=(out_spec, out_spec),
        out_shape=(jax.ShapeDtypeStruct((TOKENS, D_MODEL), _F32),
                   jax.ShapeDtypeStruct((TOKENS, D_MODEL), _BF16)),
        compiler_params=_params(("arbitrary",)),
        name="ffn_down_ln2",
    )(h, x, w_down, ln_g.reshape(DEPTH, 1, D_MODEL), ln_b.reshape(DEPTH, 1, D_MODEL))


def kernel(x, w_in, w_attn_proj, conv_dw, conv_dw_b, conv_ln_g, conv_ln_b, w_conv_proj, gmlp_ln_g, gmlp_ln_b, w_spatial, b_spatial, w_gmlp_proj, w_out, ln1_g, ln1_b, w_ffn_gate, w_ffn_up, w_ffn_down, ln2_g, ln2_b):
    assert x.shape == (BATCH, SEQ, D_MODEL) and w_in.shape == (DEPTH, D_MODEL, IN_WIDTH)
    seq_tiles = SEQ // PROJ_TM
    table_spec = pl.BlockSpec((PROJ_TM, LANES), lambda j, m: (m % seq_tiles, 0))
    tables = [(t, table_spec) for t in _rope_tables()]
    xf = x.reshape(TOKENS, D_MODEL)
    xb = None
    b_spatial_t = jnp.swapaxes(b_spatial, 1, 2)
    wa, wc, wg, wo, wd = [_to_bf16(w) for w in (w_attn_proj, w_conv_proj, w_gmlp_proj, w_out, w_ffn_down)]
    for layer in range(DEPTH):
        gmlp_vec = pl.BlockSpec((None, 1, GMLP_WIDTH), lambda j, m: (layer, 0, 0))
        hgmlp = _project(xf if xb is None else xb,
                         [(w_in, layer, T_GMLP + k, 0) for k in range(2 * GMLP_WIDTH // W_TILE)],
                         1, "gelu_sgu", "proj_gmlp", tm=SGU_TM, extra=[
                             (w_spatial, pl.BlockSpec((None, GMLP_GROUPS, GMLP_CHUNK, GMLP_CHUNK),
                                                      lambda j, m: (layer, 0, 0, 0))),
                             (b_spatial_t, pl.BlockSpec((None, GMLP_CHUNK, GMLP_GROUPS),
                                                        lambda j, m: (layer, 0, 0))),
                             (gmlp_ln_g.reshape(DEPTH, 1, GMLP_WIDTH), gmlp_vec),
                             (gmlp_ln_b.reshape(DEPTH, 1, GMLP_WIDTH), gmlp_vec)])
        if xb is None:
            hgmlp, xb = hgmlp
        qkv_groups = [
            _project(xb, [(w_in, layer, T_Q + g, 0), (w_in, layer, T_K + g, 0), (w_in, layer, T_V + g, 0)],
                     1, "qkv", f"proj_qkv_g{g}", tables, dilation=dil)
            for g, (_, dil) in enumerate(DILATED_PATTERNS)]
        h = _project(xb, [(w_in, layer, T_CONV, 1), (w_in, layer, T_CONV + CONV_WIDTH // W_TILE, 1)],
                     CONV_WIDTH // W_TILE, "glu", "proj_glu")
        gates = _project(xb, [(w_in, layer, T_GATES, 2), (w_in, layer, T_GATES + 1, 2)],
                         3 * D_MODEL // (2 * W_TILE), "sigmoid", "proj_gates")
        conv_cm = _conformer_conv(h, conv_dw, conv_dw_b, layer)
        attn = _attention(qkv_groups)
        xf, xb = _merge(attn, conv_cm, hgmlp, gates, xf, wa, wc, wg, wo,
                        conv_ln_g, conv_ln_b, ln1_g, ln1_b, layer)
        hid = _project(xb, [(w_ffn_gate, layer, 0, 1), (w_ffn_up, layer, 0, 1)],
                       FFN_HIDDEN // W_TILE, "swiglu", "ffn_up")
        xf, xb = _ffn_down(hid, xf, wd, ln2_g, ln2_b, layer)
    return xf.reshape(BATCH, SEQ, D_MODEL)
```

```python
import functools
import math

import jax
import jax.numpy as jnp
from jax import lax
from jax.experimental import pallas as pl
from jax.experimental.pallas import tpu as pltpu

D_MODEL = 2048
BATCH = 4
SEQ = 2048
DEPTH = 2
TOKENS = BATCH * SEQ
HEAD_DIM = 128
ATTN_SLOTS = 4
DILATED_PATTERNS = ((128, 1), (512, 4), (2048, 16))
N_GROUPS = len(DILATED_PATTERNS)
N_ATTN_HEADS = N_GROUPS * ATTN_SLOTS
ATTN_WIDTH = N_ATTN_HEADS * HEAD_DIM
ATTN_OUT = ATTN_SLOTS * HEAD_DIM
ATTN_BAND = 128
ROPE_THETA = 500000.0
ROPE_DIM = HEAD_DIM // 4
ROPE_HALF = ROPE_DIM // 2
CONV_WIDTH = D_MODEL // 2
CONV_KERNEL = 31
GMLP_WIDTH = D_MODEL // 2
GMLP_CHUNK = 128
GMLP_GROUPS = 8
GMLP_GROUP_CH = GMLP_WIDTH // GMLP_GROUPS
FFN_HIDDEN = -(-8 * D_MODEL // (3 * 256)) * 256
DEEPNORM_ALPHA = (2 * DEPTH) ** 0.25
LN_EPS = 1e-5
NEG_INF = -1e30
IN_WIDTH = 3 * ATTN_WIDTH + 2 * CONV_WIDTH + 2 * GMLP_WIDTH + 3 * D_MODEL

LANES = 128
SUBLANES = 8
BF16_ROWS = 16
V7X_VMEM_LIMIT_BYTES = 56 * 1024 * 1024

W_TILE = 512
PROJ_TM = 1024
T_Q, T_K, T_V = 0, ATTN_WIDTH // W_TILE, 2 * ATTN_WIDTH // W_TILE
T_CONV = 3 * ATTN_WIDTH // W_TILE
T_GMLP = T_CONV + 2 * CONV_WIDTH // W_TILE
T_GATES = T_GMLP + 2 * GMLP_WIDTH // W_TILE
ATTN_TQ = 128
GATE_TILES = 3
REGROUP_STRIDE = 4
CONV_TS = 256
CONV_HALO = 32
CONV_ROWS = 128
SGU_TM = 512
MERGE_TM = 256
DOWN_TM = 256
assert W_TILE == ATTN_SLOTS * HEAD_DIM and SEQ % PROJ_TM == 0
assert all(window // dil == ATTN_BAND and PROJ_TM % dil == 0 for window, dil in DILATED_PATTERNS)

_F32 = jnp.float32
_BF16 = jnp.bfloat16


def _params(semantics):
    return pltpu.CompilerParams(dimension_semantics=semantics,
                                vmem_limit_bytes=V7X_VMEM_LIMIT_BYTES)


def _layer_norm(x, g, b):
    mu = jnp.mean(x, axis=-1, keepdims=True)
    xc = x - mu
    var = jnp.mean(xc * xc, axis=-1, keepdims=True)
    return xc * lax.rsqrt(var + LN_EPS) * g + b


def _silu(x):
    h = 0.5 * x
    return h + h * jnp.tanh(h)


def _gelu(x):
    return 0.5 * x * (1.0 + lax.erf(x * math.sqrt(0.5)))


def _rope_table_kernel(cos_ref, sin_lo_ref, sin_hi_ref):
    pos = lax.broadcasted_iota(jnp.int32, (SEQ, LANES), 0).astype(_F32)
    lane = lax.broadcasted_iota(jnp.int32, (SEQ, LANES), 1)
    fidx = (lane & (ROPE_HALF - 1)).astype(_F32)
    inv_freq = jnp.exp(fidx * (-math.log(ROPE_THETA) / ROPE_HALF))
    ang = pos * inv_freq
    c = jnp.cos(ang)
    s = jnp.sin(ang)
    cos_ref[...] = jnp.where(lane < ROPE_DIM, c, 1.0)
    sin_lo_ref[...] = jnp.where(lane < ROPE_HALF, -s, 0.0)
    sin_hi_ref[...] = jnp.where((lane >= ROPE_HALF) & (lane < ROPE_DIM), s, 0.0)


def _rope_tables():
    shp = jax.ShapeDtypeStruct((SEQ, LANES), _F32)
    return pl.pallas_call(_rope_table_kernel, out_shape=(shp, shp, shp), name="rope_tables")()


def _regroup_stages(dilation):
    return 0 if dilation == 1 else 1 if dilation <= REGROUP_STRIDE else 2


def _proj_kernel(*refs, n_w, n_extra, n_side, epilogue, dilation=1, emit_x_bf16=False):
    x_ref = refs[0]
    w_refs = refs[1:1 + n_w]
    extra = refs[1 + n_w:1 + n_w + n_extra]
    side_in = refs[1 + n_w + n_extra:1 + n_w + n_extra + n_side]
    outs = refs[1 + n_w + n_extra + n_side:]
    o_ref = outs[0]
    xb_ref = outs[1] if emit_x_bf16 else None
    side_out = outs[1 + emit_x_bf16:1 + emit_x_bf16 + n_side]
    wbf_ref, *stage_refs = outs[1 + emit_x_bf16 + n_side:]

    half = n_w // 2 * W_TILE
    halved = {"sigmoid": range(n_w), "glu": range(n_w // 2, n_w), "swiglu": range(n_w // 2)
              }.get(epilogue, ())

    @pl.when(pl.program_id(1) == 0)
    def _():
        for k, w_ref in enumerate(w_refs):
            w = w_ref[...] * 0.5 if k in halved else w_ref[...]
            wbf_ref[:, k * W_TILE:(k + 1) * W_TILE] = w.astype(_BF16)

    for src, dst in zip(side_in, side_out):
        dst[...] = src[...].astype(dst.dtype)
    if emit_x_bf16:
        x = x_ref[...].astype(_BF16)
        xb_ref[...] = x
    else:
        x = x_ref[...]
    acc = jnp.dot(x, wbf_ref[...], preferred_element_type=_F32)
    if epilogue == "gelu_sgu":
        ws_ref, bs_ref, g_ref, b_ref = extra
        z = _gelu(acc)
        vn = _layer_norm(z[:, GMLP_WIDTH:], g_ref[...], b_ref[...]).astype(_BF16)
        chunks = x_ref.shape[0] // GMLP_CHUNK
        row = lax.broadcasted_iota(jnp.int32, (GMLP_CHUNK, GMLP_CHUNK), 0)
        col = lax.broadcasted_iota(jnp.int32, (GMLP_CHUNK, GMLP_CHUNK), 1)
        for g in range(GMLP_GROUPS):
            cols = slice(g * GMLP_GROUP_CH, (g + 1) * GMLP_GROUP_CH)
            wm = jnp.where(col <= row, ws_ref[g], 0.0).astype(_BF16)
            rhs = jnp.concatenate(
                [vn[n * GMLP_CHUNK:(n + 1) * GMLP_CHUNK, cols] for n in range(chunks)], axis=1)
            s = jnp.dot(wm, rhs, preferred_element_type=_F32) + bs_ref[:, g:g + 1]
            for n in range(chunks):
                rows = slice(n * GMLP_CHUNK, (n + 1) * GMLP_CHUNK)
                o_ref[rows, cols] = (z[rows, cols] * s[:, n * GMLP_GROUP_CH:(n + 1) * GMLP_GROUP_CH]
                                     ).astype(o_ref.dtype)
        return
    elif epilogue == "sigmoid":
        out = 0.5 * jnp.tanh(acc) + 0.5
    elif epilogue == "glu":
        out = acc[:, :half] * (0.5 * jnp.tanh(acc[:, half:]) + 0.5)
        for k in range(half // LANES):
            o_ref[k] = out[:, k * LANES:(k + 1) * LANES].astype(o_ref.dtype)
        return
    elif epilogue == "swiglu":
        h = acc[:, :half]
        out = (h + h * jnp.tanh(h)) * acc[:, half:]
    elif epilogue == "qkv":
        cos_ref, sin_lo_ref, sin_hi_ref = extra
        c, s_lo, s_hi = cos_ref[...], sin_lo_ref[...], sin_hi_ref[...]
        rows_per_class = PROJ_TM // dilation
        for h in range(n_w * W_TILE // HEAD_DIM):
            cols = slice(h * HEAD_DIM, (h + 1) * HEAD_DIM)
            r = acc[:, cols]
            if h < 2 * ATTN_SLOTS:
                r = (r * c + pltpu.roll(r, HEAD_DIM - ROPE_HALF, 1) * s_lo
                     + pltpu.roll(r, ROPE_HALF, 1) * s_hi)
            if h < ATTN_SLOTS:
                r = r * HEAD_DIM ** -0.5
            if dilation == 1:
                o_ref[:, cols] = r.astype(o_ref.dtype)
            elif dilation <= REGROUP_STRIDE:
                stage_ref, = stage_refs
                stage_ref[h] = r
                for cls in range(dilation):
                    o_ref[cls * rows_per_class:(cls + 1) * rows_per_class, cols] = (
                        stage_ref[h, pl.ds(cls, rows_per_class, stride=dilation), :].astype(o_ref.dtype))
            else:
                stage_ref, stage2_ref = stage_refs
                stage_ref[h] = r
                outer = dilation // REGROUP_STRIDE
                rows_inner = PROJ_TM // REGROUP_STRIDE
                for c_in in range(REGROUP_STRIDE):
                    stage2_ref[h, c_in * rows_inner:(c_in + 1) * rows_inner, :] = (
                        stage_ref[h, pl.ds(c_in, rows_inner, stride=REGROUP_STRIDE), :])
                for c_in in range(REGROUP_STRIDE):
                    for k in range(outer):
                        cls = c_in + REGROUP_STRIDE * k
                        o_ref[cls * rows_per_class:(cls + 1) * rows_per_class, cols] = (
                            stage2_ref[h, pl.ds(c_in * rows_inner + k, rows_per_class, stride=outer), :]
                            .astype(o_ref.dtype))
        return
    else:
        raise ValueError(epilogue)
    o_ref[...] = out.astype(o_ref.dtype)


def _project(xb, weights, n_steps, epilogue, name, extra=(), dilation=1, tm=PROJ_TM, side_casts=()):
    n_w = len(weights)
    w_mode = dict(pipeline_mode=pl.Buffered(1)) if n_steps == 1 else {}
    scratch = [pltpu.VMEM((D_MODEL, n_w * W_TILE), _BF16)]
    if epilogue == "qkv":
        assert tm == PROJ_TM
        scratch += [pltpu.VMEM((n_w * W_TILE // HEAD_DIM, PROJ_TM, HEAD_DIM), _F32)
                    ] * _regroup_stages(dilation)
    out_tn = {"glu": n_w // 2 * W_TILE, "swiglu": n_w // 2 * W_TILE, "gelu_sgu": GMLP_WIDTH
              }.get(epilogue, n_w * W_TILE)
    if epilogue == "glu":
        chunks = out_tn // LANES
        out_spec = pl.BlockSpec((chunks, tm, LANES), lambda j, m: (j, m, 0))
        out_shape = jax.ShapeDtypeStruct((n_steps * chunks, TOKENS, LANES), _BF16)
    else:
        out_spec = pl.BlockSpec((tm, out_tn), lambda j, m: (m, j))
        out_shape = jax.ShapeDtypeStruct((TOKENS, n_steps * out_tn), _BF16)
    w_specs = [pl.BlockSpec((None, D_MODEL, W_TILE),
                            lambda j, m, layer=layer, first=first, stride=stride:
                            (layer, 0, first + stride * j), **w_mode)
               for _, layer, first, stride in weights]
    out_specs, out_shapes = [out_spec], [out_shape]
    emit_x_bf16 = xb.dtype != _BF16
    if emit_x_bf16:
        assert n_steps == 1
        out_specs.append(pl.BlockSpec((tm, D_MODEL), lambda j, m: (m, 0)))
        out_shapes.append(jax.ShapeDtypeStruct((TOKENS, D_MODEL), _BF16))
    token_tiles = TOKENS // tm
    side_specs = []
    for w, layer in side_casts:
        _, k, n = w.shape
        rows = k // (n_steps * token_tiles)
        assert rows * n_steps * token_tiles == k and rows % BF16_ROWS == 0
        side_specs.append(pl.BlockSpec((None, rows, n),
                                       lambda j, m, layer=layer: (layer, j * token_tiles + m, 0)))
        out_specs.append(pl.BlockSpec((rows, n), lambda j, m: (j * token_tiles + m, 0)))
        out_shapes.append(jax.ShapeDtypeStruct((k, n), _BF16))
    single = len(out_specs) == 1
    return pl.pallas_call(
        functools.partial(_proj_kernel, n_w=n_w, n_extra=len(extra), n_side=len(side_casts),
                          epilogue=epilogue, dilation=dilation, emit_x_bf16=emit_x_bf16),
        grid=(n_steps, token_tiles),
        in_specs=([pl.BlockSpec((tm, D_MODEL), lambda j, m: (m, 0))] + w_specs
                  + [s for _, s in extra] + side_specs),
        out_specs=out_specs[0] if single else tuple(out_specs),
        out_shape=out_shapes[0] if single else tuple(out_shapes),
        scratch_shapes=scratch,
        compiler_params=_params(("arbitrary", "arbitrary")),
        name=name,
    )(xb, *[w for w, _, _, _ in weights], *[a for a, _ in extra], *[w for w, _ in side_casts])


def _attn_kernel(q0, k0, v0, q1, k1, v1, q2, k2, v2, o_ref, *scratch):
    groups = ((q0, k0, v0), (q1, k1, v1), (q2, k2, v2))
    stats = scratch[:3 * N_GROUPS]
    vcs = scratch[3 * N_GROUPS:]
    qi = lax.broadcasted_iota(jnp.int32, (ATTN_TQ, 2 * ATTN_BAND), 0)
    kj = lax.broadcasted_iota(jnp.int32, (ATTN_TQ, 2 * ATTN_BAND), 1)
    band_mask = (kj >= qi) & (kj <= qi + ATTN_BAND)
    causal_mask = (lax.broadcasted_iota(jnp.int32, (ATTN_TQ, ATTN_BAND), 1)
                   <= lax.broadcasted_iota(jnp.int32, (ATTN_TQ, ATTN_BAND), 0))

    def class_rows(ref, dil, cls, start, count):
        per_tile = PROJ_TM // dil
        pieces, pos = [], start
        while pos < start + count:
            tile, off = divmod(pos, per_tile)
            take = min(per_tile - off, start + count - pos)
            row = tile * PROJ_TM + cls * per_tile + off
            pieces.append(ref[row:row + take, :])
            pos += take
        return pieces[0] if len(pieces) == 1 else jnp.concatenate(pieces, axis=0)

    for g, (window, dil) in enumerate(DILATED_PATTERNS):
        q_ref, k_ref, v_ref = groups[g]
        m_ref, den_ref, pv_ref = stats[3 * g:3 * g + 3]
        vc_ref = vcs[g]
        vc_ref[:, HEAD_DIM:] = jnp.ones((SEQ, HEAD_DIM), _BF16)
        vc_ref[:, :HEAD_DIM] = v_ref[...]
        length = SEQ // dil
        for c in range(dil):
            for i in range(length // ATTN_TQ):
                row0 = i * ATTN_TQ
                k_lo = row0 if i == 0 else row0 - ATTN_BAND
                span = row0 + ATTN_TQ - k_lo
                q = class_rows(q_ref, dil, c, row0, ATTN_TQ)
                k = class_rows(k_ref, dil, c, k_lo, span)
                s = lax.dot_general(q, k, (((1,), (1,)), ((), ())), preferred_element_type=_F32)
                s = jnp.where(causal_mask if i == 0 else band_mask, s, NEG_INF)
                m = jnp.max(s, axis=-1, keepdims=True)
                e = jnp.exp(s - m).astype(_BF16)
                r = jnp.dot(e, class_rows(vc_ref, dil, c, k_lo, span), preferred_element_type=_F32)
                if dil == 1:
                    tok = pl.ds(row0, ATTN_TQ)
                else:
                    tok = pl.ds(c + dil * i * ATTN_TQ, ATTN_TQ, stride=dil)
                m_ref[tok, :] = jnp.broadcast_to(m, (ATTN_TQ, HEAD_DIM))
                den_ref[tok, :] = r[:, HEAD_DIM:]
                pv_ref[tok, :] = r[:, :HEAD_DIM]

    rows_per_step = 2 * ATTN_TQ
    for t in range(SEQ // rows_per_step):
        rows = slice(t * rows_per_step, (t + 1) * rows_per_step)
        ms = [stats[3 * g][rows, :] for g in range(N_GROUPS)]
        m_all = functools.reduce(jnp.maximum, ms)
        num = jnp.zeros((rows_per_step, HEAD_DIM), _F32)
        z = jnp.zeros((rows_per_step, HEAD_DIM), _F32)
        for g in range(N_GROUPS):
            w = jnp.exp(ms[g] - m_all)
            z = z + w * stats[3 * g + 1][rows, :]
            num = num + w * stats[3 * g + 2][rows, :]
        o_ref[rows, :] = (num / z).astype(o_ref.dtype)


def _attention(qkv_groups):
    in_specs, operands = [], []
    for qkv in qkv_groups:
        for part in range(3):
            in_specs.append(pl.BlockSpec(
                (SEQ, HEAD_DIM), lambda b, s, part=part: (b, part * ATTN_SLOTS + s)))
            operands.append(qkv)
    tok_f32 = pltpu.VMEM((SEQ, HEAD_DIM), _F32)
    return pl.pallas_call(
        _attn_kernel,
        grid=(BATCH, ATTN_SLOTS),
        in_specs=in_specs,
        out_specs=pl.BlockSpec((SEQ, HEAD_DIM), lambda b, s: (b, s)),
        out_shape=jax.ShapeDtypeStruct((TOKENS, ATTN_OUT), _BF16),
        scratch_shapes=([tok_f32] * (3 * N_GROUPS)
                        + [pltpu.VMEM((SEQ, 2 * HEAD_DIM), _BF16)] * N_GROUPS),
        compiler_params=_params(("arbitrary", "arbitrary")),
        name="dilated_attention",
    )(*operands)


def _conv_block(c, base, cw_ref, cb_ref, cv_ref, hp_ref):
    off = CONV_HALO - (CONV_KERNEL - 1)
    vregs_per_block = CONV_ROWS // SUBLANES
    accs = [jnp.broadcast_to(cb_ref[c], (SUBLANES, LANES))] * vregs_per_block
    for j in range(CONV_KERNEL):
        wj = jnp.broadcast_to(cw_ref[c, j:j + 1, :], (SUBLANES, LANES))
        for v in range(vregs_per_block):
            accs[v] = accs[v] + wj * hp_ref[c, pl.ds(base + off + j + v * SUBLANES, SUBLANES), :]
    for v in range(vregs_per_block):
        cv_ref[c, pl.ds(base + v * SUBLANES, SUBLANES), :] = accs[v]


def _conv_kernel(h_ref, ph_ref, cw_ref, cb_ref, cv_ref, hp_ref):
    first = pl.program_id(1) == 0
    blocks_per_chunk = CONV_TS // CONV_ROWS

    def fill(c, carry):
        hp_ref[c, 0:CONV_HALO, :] = jnp.where(first, 0.0, ph_ref[c].astype(_F32))
        hp_ref[c, CONV_HALO:, :] = h_ref[c].astype(_F32)
        return carry

    lax.fori_loop(0, CONV_WIDTH // LANES, fill, 0)

    def block(i, carry):
        _conv_block(i // blocks_per_chunk, (i % blocks_per_chunk) * CONV_ROWS,
                    cw_ref, cb_ref, cv_ref, hp_ref)
        return carry

    lax.fori_loop(0, (CONV_WIDTH // LANES) * blocks_per_chunk, block, 0)


def _conformer_conv(h_cm, w_dw, b_dw, layer):
    n_chunks = CONV_WIDTH // LANES
    tiles = SEQ // CONV_TS
    halo_per_tile = CONV_TS // CONV_HALO
    w_chunks = jnp.swapaxes(w_dw.reshape(DEPTH, CONV_KERNEL, n_chunks, LANES), 1, 2)
    b_chunks = b_dw.reshape(DEPTH, n_chunks, 1, LANES)
    return pl.pallas_call(
        _conv_kernel,
        grid=(BATCH, tiles),
        in_specs=[pl.BlockSpec((n_chunks, CONV_TS, LANES), lambda b, i: (0, b * tiles + i, 0)),
                  pl.BlockSpec((n_chunks, CONV_HALO, LANES),
                               lambda b, i: (0, jnp.maximum((b * tiles + i) * halo_per_tile - 1, 0), 0)),
                  pl.BlockSpec((None, n_chunks, CONV_KERNEL, LANES), lambda b, i: (layer, 0, 0, 0)),
                  pl.BlockSpec((None, n_chunks, 1, LANES), lambda b, i: (layer, 0, 0, 0))],
        out_specs=pl.BlockSpec((n_chunks, CONV_TS, LANES), lambda b, i: (0, b * tiles + i, 0)),
        out_shape=jax.ShapeDtypeStruct((n_chunks, TOKENS, LANES), _F32),
        scratch_shapes=[pltpu.VMEM((n_chunks, CONV_HALO + CONV_TS, LANES), _F32)],
        compiler_params=_params(("arbitrary", "arbitrary")),
        name="conformer_conv",
    )(h_cm, h_cm, w_chunks, b_chunks)


def _merge_kernel(attn_ref, cv_ref, hg_ref, ga_ref, gc_ref, gm_ref, x_ref,
                  wa_ref, wc_ref, wg_ref, wo_ref, cg_ref, cb_ref, g_ref, b_ref, xo_ref, xob_ref):
    conv = jnp.concatenate([cv_ref[c] for c in range(CONV_WIDTH // LANES)], axis=1)
    hc = _layer_norm(conv, cg_ref[...], cb_ref[...])
    hc = _silu(hc).astype(_BF16)
    ya = jnp.dot(attn_ref[...], wa_ref[...], preferred_element_type=_F32)
    merged = ga_ref[...].astype(_F32) * ya
    yc = jnp.dot(hc, wc_ref[...], preferred_element_type=_F32)
    merged = merged + gc_ref[...].astype(_F32) * yc
    yg = jnp.dot(hg_ref[...], wg_ref[...], preferred_element_type=_F32)
    merged = merged + gm_ref[...].astype(_F32) * yg
    y = jnp.dot(merged.astype(_BF16), wo_ref[...], preferred_element_type=_F32)
    out = _layer_norm(DEEPNORM_ALPHA * x_ref[...] + y, g_ref[...], b_ref[...])
    xo_ref[...] = out
    xob_ref[...] = out.astype(_BF16)


def _resident(shape):
    return pl.BlockSpec(shape, lambda i: (0, 0), pipeline_mode=pl.Buffered(1))


def _merge(attn, conv_cm, hgmlp, gates, x, wa, wc, wg, wo, conv_ln_g, conv_ln_b, ln_g, ln_b, layer):
    tm = MERGE_TM
    row = lambda width: pl.BlockSpec((tm, width), lambda i: (i, 0))
    gate = lambda k: pl.BlockSpec((tm, D_MODEL), lambda i: (i, k))
    vec = lambda width: pl.BlockSpec((None, 1, width), lambda i: (layer, 0, 0))
    out_spec = pl.BlockSpec((tm, D_MODEL), lambda i: (i, 0))
    return pl.pallas_call(
        _merge_kernel,
        grid=(TOKENS // tm,),
        in_specs=[row(ATTN_OUT),
                  pl.BlockSpec((CONV_WIDTH // LANES, tm, LANES), lambda i: (0, i, 0)),
                  row(GMLP_WIDTH), gate(0), gate(1), gate(2), row(D_MODEL),
                  _resident((ATTN_OUT, D_MODEL)), _resident((CONV_WIDTH, D_MODEL)),
                  _resident((GMLP_WIDTH, D_MODEL)), _resident((D_MODEL, D_MODEL)),
                  vec(CONV_WIDTH), vec(CONV_WIDTH), vec(D_MODEL), vec(D_MODEL)],
        out_specs=(out_spec, out_spec),
        out_shape=(jax.ShapeDtypeStruct((TOKENS, D_MODEL), _F32),
                   jax.ShapeDtypeStruct((TOKENS, D_MODEL), _BF16)),
        compiler_params=_params(("arbitrary",)),
        name="merge_out_ln1",
    )(attn, conv_cm, hgmlp, gates, gates, gates, x, wa, wc, wg, wo,
      conv_ln_g.reshape(DEPTH, 1, CONV_WIDTH), conv_ln_b.reshape(DEPTH, 1, CONV_WIDTH),
      ln_g.reshape(DEPTH, 1, D_MODEL), ln_b.reshape(DEPTH, 1, D_MODEL))


def _ffn_down_kernel(h_ref, x_ref, w_ref, g_ref, b_ref, xo_ref, xob_ref):
    f = jnp.dot(h_ref[...], w_ref[...], preferred_element_type=_F32)
    out = _layer_norm(DEEPNORM_ALPHA * x_ref[...] + f, g_ref[...], b_ref[...])
    xo_ref[...] = out
    xob_ref[...] = out.astype(_BF16)


def _ffn_down(h, x, w_down, ln_g, ln_b, layer):
    tm = DOWN_TM
    vec = pl.BlockSpec((None, 1, D_MODEL), lambda i: (layer, 0, 0))
    out_spec = pl.BlockSpec((tm, D_MODEL), lambda i: (i, 0))
    return pl.pallas_call(
        _ffn_down_kernel,
        grid=(TOKENS // tm,),
        in_specs=[pl.BlockSpec((tm, FFN_HIDDEN), lambda i: (i, 0)),
                  pl.BlockSpec((tm, D_MODEL), lambda i: (i, 0)),
                  _resident((FFN_HIDDEN, D_MODEL)), vec, vec],
        out_specs=(out_spec, out_spec),
        out_shape=(jax.ShapeDtypeStruct((TOKENS, D_MODEL), _F32),
                   jax.ShapeDtypeStruct((TOKENS, D_MODEL), _BF16)),
        compiler_params=_params(("arbitrary",)),
        name="ffn_down_ln2",
    )(h, x, w_down, ln_g.reshape(DEPTH, 1, D_MODEL), ln_b.reshape(DEPTH, 1, D_MODEL))


def kernel(x, w_in, w_attn_proj, conv_dw, conv_dw_b, conv_ln_g, conv_ln_b, w_conv_proj, gmlp_ln_g, gmlp_ln_b, w_spatial, b_spatial, w_gmlp_proj, w_out, ln1_g, ln1_b, w_ffn_gate, w_ffn_up, w_ffn_down, ln2_g, ln2_b):
    assert x.shape == (BATCH, SEQ, D_MODEL) and w_in.shape == (DEPTH, D_MODEL, IN_WIDTH)
    seq_tiles = SEQ // PROJ_TM
    table_spec = pl.BlockSpec((PROJ_TM, LANES), lambda j, m: (m % seq_tiles, 0))
    tables = [(t, table_spec) for t in _rope_tables()]
    xf = x.reshape(TOKENS, D_MODEL)
    xb = None
    b_spatial_t = jnp.swapaxes(b_spatial, 1, 2)
    for layer in range(DEPTH):
        gmlp_vec = pl.BlockSpec((None, 1, GMLP_WIDTH), lambda j, m: (layer, 0, 0))
        hgmlp = _project(xf if xb is None else xb,
                         [(w_in, layer, T_GMLP + k, 0) for k in range(2 * GMLP_WIDTH // W_TILE)],
                         1, "gelu_sgu", "proj_gmlp", tm=SGU_TM, extra=[
                             (w_spatial, pl.BlockSpec((None, GMLP_GROUPS, GMLP_CHUNK, GMLP_CHUNK),
                                                      lambda j, m: (layer, 0, 0, 0))),
                             (b_spatial_t, pl.BlockSpec((None, GMLP_CHUNK, GMLP_GROUPS),
                                                        lambda j, m: (layer, 0, 0))),
                             (gmlp_ln_g.reshape(DEPTH, 1, GMLP_WIDTH), gmlp_vec),
                             (gmlp_ln_b.reshape(DEPTH, 1, GMLP_WIDTH), gmlp_vec)])
        if xb is None:
            hgmlp, xb = hgmlp
        qkv_groups = [
            _project(xb, [(w_in, layer, T_Q + g, 0), (w_in, layer, T_K + g, 0), (w_in, layer, T_V + g, 0)],
                     1, "qkv", f"proj_qkv_g{g}", tables, dilation=dil)
            for g, (_, dil) in enumerate(DILATED_PATTERNS)]
        h = _project(xb, [(w_in, layer, T_CONV, 1), (w_in, layer, T_CONV + CONV_WIDTH // W_TILE, 1)],
                     CONV_WIDTH // W_TILE, "glu", "proj_glu")
        gates, wa, wc, wg, wo = _project(
            xb, [(w_in, layer, T_GATES + k, GATE_TILES) for k in range(GATE_TILES)],
            3 * D_MODEL // (GATE_TILES * W_TILE), "sigmoid", "proj_gates",
            side_casts=[(w, layer) for w in (w_attn_proj, w_conv_proj, w_gmlp_proj, w_out)])
        conv_cm = _conformer_conv(h, conv_dw, conv_dw_b, layer)
        attn = _attention(qkv_groups)
        xf, xb = _merge(attn, conv_cm, hgmlp, gates, xf, wa, wc, wg, wo,
                        conv_ln_g, conv_ln_b, ln1_g, ln1_b, layer)
        hid, wd = _project(xb, [(w_ffn_gate, layer, 0, 1), (w_ffn_up, layer, 0, 1)],
                           FFN_HIDDEN // W_TILE, "swiglu", "ffn_up", side_casts=[(w_ffn_down, layer)])
        xf, xb = _ffn_down(hid, xf, wd, ln2_g, ln2_b, layer)
    return xf.reshape(BATCH, SEQ, D_MODEL)
```
